```python
import math
import jax, jax.numpy as jnp
from jax import lax
import numpy as np

D_MODEL = 4096
BATCH = 4
SEQ = 4096
DEPTH = 4

N_EVEN = (DEPTH + 1) // 2
N_ODD = DEPTH // 2
NORM_EPS = 1e-6
ADA_RANK = 256
N_MOD = 6
ATT_HEADS = D_MODEL // 256
ATT_HD = 64
ATT_VD = 2 * ATT_HD
ATT_QK = ATT_HEADS * 2 * ATT_HD
ATT_V = ATT_HEADS * ATT_VD
Q_BLOCK = 128
SGU_GROUPS = D_MODEL // 256
SGU_CH = 128
SGU_W = SGU_GROUPS * SGU_CH
SGU_CHUNK = 128
HYB_IN = 2 * ATT_QK + ATT_V + 2 * SGU_W
HYB_OUT = ATT_V + SGU_W
D_INNER = 2 * D_MODEL
SSD_HD = 64
SSD_HEADS = D_INNER // SSD_HD
SSD_GROUPS = 8
SSD_HPG = SSD_HEADS // SSD_GROUPS
SSD_STATE = 128
CONV_K = 4
SSD_CHUNK = 128
CONV_DIM = D_INNER + 2 * SSD_GROUPS * SSD_STATE
SSD_IN = D_INNER + CONV_DIM + SSD_HEADS
PEER_HEADS = 8
N_KEYS = 128
N_EXPERTS = N_KEYS * N_KEYS
PEER_HALF = 128
PEER_TOPK = 16
PEER_BLOCK = 128

kernel_name = 'hybrid_diffattn_sgu_ssd_peer_adaln'


def _rmsnorm(x, g):
    xf = x.astype(jnp.float32)
    y = xf * lax.rsqrt(jnp.mean(xf * xf, axis=-1, keepdims=True) + NORM_EPS)
    return (y * g.astype(jnp.float32)).astype(x.dtype)


def _layernorm(x, g, b):
    xf = x.astype(jnp.float32)
    mu = jnp.mean(xf, axis=-1, keepdims=True)
    var = jnp.mean(jnp.square(xf - mu), axis=-1, keepdims=True)
    y = (xf - mu) * lax.rsqrt(var + NORM_EPS)
    return (y * g.astype(jnp.float32) + b.astype(jnp.float32)).astype(x.dtype)


def _diff_attention(q, k, v, lam, lam_init, subln_g):
    bsz, seq = q.shape[:2]
    nb = seq // Q_BLOCK
    scale = ATT_HD ** -0.5
    q_blocks = q.reshape(bsz, nb, Q_BLOCK, ATT_HEADS, 2, ATT_HD).swapaxes(0, 1)
    starts = jnp.arange(nb, dtype=jnp.int32) * Q_BLOCK
    k_pos = jnp.arange(seq, dtype=jnp.int32)

    def one_block(args):
        q_blk, start = args
        s = jnp.einsum('bqhcd,bkhcd->bhcqk', q_blk, k).astype(jnp.float32) * scale
        q_pos = start + jnp.arange(Q_BLOCK, dtype=jnp.int32)
        causal = k_pos[None, :] <= q_pos[:, None]
        s = jnp.where(causal, s, -jnp.inf)
        p = jax.nn.softmax(s, axis=-1)
        w = p[:, :, 0] - lam * p[:, :, 1]
        return jnp.einsum('bhqk,bkhd->bqhd', w.astype(v.dtype), v)

    o = lax.map(one_block, (q_blocks, starts))
    o = o.swapaxes(0, 1).reshape(bsz, seq, ATT_HEADS, ATT_VD)
    o = _rmsnorm(o, subln_g) * (1.0 - lam_init)
    return o.reshape(bsz, seq, ATT_V)


def _chunked_sgu(u, v, ln_g, ln_b, w_s, b_s):
    bsz, seq = u.shape[:2]
    nc = seq // SGU_CHUNK
    u = jax.nn.gelu(u)
    v = _layernorm(jax.nn.gelu(v), ln_g, ln_b)
    v = v.reshape(bsz, nc, SGU_CHUNK, SGU_GROUPS, SGU_CH)
    causal = jnp.tril(jnp.ones((SGU_CHUNK, SGU_CHUNK), dtype=bool))
    w = jnp.where(causal[None], w_s, jnp.zeros((), w_s.dtype))
    mixed = jnp.einsum('gij,bnjgc->bnigc', w, v) + b_s.T[:, :, None]
    return (u.reshape(mixed.shape) * mixed).reshape(bsz, seq, SGU_W)


def _attn_sgu_mixer(h, w_in, w_out, lam_p, subln_g, ln_g, ln_b, w_s, b_s, lam_init):
    bsz, seq = h.shape[:2]
    proj = h @ w_in
    q, k, v, u, gv = jnp.split(
        proj, [ATT_QK, 2 * ATT_QK, 2 * ATT_QK + ATT_V, 2 * ATT_QK + ATT_V + SGU_W], axis=-1)
    q = q.reshape(bsz, seq, ATT_HEADS, 2, ATT_HD)
    k = k.reshape(bsz, seq, ATT_HEADS, 2, ATT_HD)
    v = v.reshape(bsz, seq, ATT_HEADS, ATT_VD)
    lp = lam_p.astype(jnp.float32)
    lam = jnp.exp(jnp.sum(lp[0] * lp[1])) - jnp.exp(jnp.sum(lp[2] * lp[3])) + lam_init
    a_out = _diff_attention(q, k, v, lam, lam_init, subln_g)
    s_out = _chunked_sgu(u, gv, ln_g, ln_b, w_s, b_s)
    return jnp.concatenate([a_out, s_out], axis=-1) @ w_out


def _causal_dwconv(x, w, b):
    rhs = w[:, None, :]
    y = lax.conv_general_dilated(x, rhs, window_strides=(1,), padding=[(CONV_K - 1, 0)],
                                 dimension_numbers=('NWC', 'WIO', 'NWC'),
                                 feature_group_count=x.shape[-1])
    return y + b


def _ssd_scan(x, dt, a, b_in, c_in):
    bsz, seq = x.shape[:2]
    nc = seq // SSD_CHUNK

    def chunks(t):
        return t.reshape((bsz, nc, SSD_CHUNK) + t.shape[2:]).swapaxes(0, 1)

    xs = chunks(x.reshape(bsz, seq, SSD_GROUPS, SSD_HPG, SSD_HD))
    dts = chunks(dt.reshape(bsz, seq, SSD_GROUPS, SSD_HPG))
    bs = chunks(b_in)
    cs = chunks(c_in)
    a_g = a.reshape(SSD_GROUPS, SSD_HPG)
    causal = jnp.tril(jnp.ones((SSD_CHUNK, SSD_CHUNK), dtype=bool))[None, :, :, None, None]

    def step(state, inp):
        xc, dtc, bc, cc = inp
        acs = jnp.cumsum(dtc * a_g, axis=1)
        seg = acs[:, :, None] - acs[:, None, :]
        decay = jnp.exp(jnp.where(causal, seg, -jnp.inf))
        xdt = xc * dtc[..., None]
        cb = jnp.einsum('bign,bjgn->bijg', cc, bc)
        y = jnp.einsum('bijgr,bjgrp->bigrp', cb[..., None] * decay, xdt)
        y = y + jnp.einsum('bign,bgrpn->bigrp', cc, state) * jnp.exp(acs)[..., None]
        tail = jnp.exp(acs[:, -1:] - acs)
        state = (state * jnp.exp(acs[:, -1])[..., None, None]
                 + jnp.einsum('bjgr,bjgrp,bjgn->bgrpn', tail, xdt, bc))
        return state, y

    state0 = jnp.zeros((bsz, SSD_GROUPS, SSD_HPG, SSD_HD, SSD_STATE), jnp.float32)
    _, ys = lax.scan(step, state0, (xs, dts, bs, cs))
    return ys.swapaxes(0, 1).reshape(bsz, seq, SSD_HEADS, SSD_HD)


def _ssd_mixer(h, w_in, conv_w, conv_b, dt_bias, a_log, d_skip, norm_g, w_out):
    bsz, seq = h.shape[:2]
    f32 = jnp.float32
    z, xbc, dt = jnp.split(h @ w_in, [D_INNER, D_INNER + CONV_DIM], axis=-1)
    xbc = jax.nn.silu(_causal_dwconv(xbc, conv_w, conv_b))
    xs, b_in, c_in = jnp.split(xbc, [D_INNER, D_INNER + SSD_GROUPS * SSD_STATE], axis=-1)
    dt = jax.nn.softplus(dt.astype(f32) + dt_bias.astype(f32))
    a = -jnp.exp(a_log.astype(f32))
    xs = xs.reshape(bsz, seq, SSD_HEADS, SSD_HD).astype(f32)
    y = _ssd_scan(xs, dt, a,
                  b_in.reshape(bsz, seq, SSD_GROUPS, SSD_STATE).astype(f32),
                  c_in.reshape(bsz, seq, SSD_GROUPS, SSD_STATE).astype(f32))
    y = y + xs * d_skip.astype(f32)[:, None]
    y = y.reshape(bsz, seq, D_INNER) * jax.nn.silu(z.astype(f32))
    y = y.reshape(bsz, seq, SSD_GROUPS, D_INNER // SSD_GROUPS)
    y = y * lax.rsqrt(jnp.mean(y * y, axis=-1, keepdims=True) + NORM_EPS)
    y = y.reshape(bsz, seq, D_INNER) * norm_g.astype(f32)
    return y.astype(h.dtype) @ w_out


def _peer(h, w_q, keys, u_tab, v_tab):
    bsz, seq, dm = h.shape
    t = bsz * seq
    ht = h.reshape(t, dm)
    q = (ht @ w_q).reshape(t, PEER_HEADS, 2, PEER_HALF)
    s = jnp.einsum('thcd,hckd->thck', q, keys).astype(jnp.float32)
    s1, i1 = lax.top_k(s[:, :, 0], PEER_TOPK)
    s2, i2 = lax.top_k(s[:, :, 1], PEER_TOPK)
    cand_s = (s1[..., :, None] + s2[..., None, :]).reshape(t, PEER_HEADS, PEER_TOPK * PEER_TOPK)
    cand_i = (i1[..., :, None] * N_KEYS + i2[..., None, :]).reshape(t, PEER_HEADS, PEER_TOPK * PEER_TOPK)
    top_s, pos = lax.top_k(cand_s, PEER_TOPK)
    experts = jnp.take_along_axis(cand_i, pos, axis=-1)
    gates = jax.nn.softmax(top_s, axis=-1).astype(h.dtype)
    nblk = t // PEER_BLOCK
    experts = experts.reshape(nblk, PEER_BLOCK, PEER_HEADS * PEER_TOPK)
    gates = gates.reshape(nblk, PEER_BLOCK, PEER_HEADS * PEER_TOPK)
    hb = ht.reshape(nblk, PEER_BLOCK, dm)

    def one_block(args):
        h_blk, e_blk, g_blk = args
        ue = jnp.take(u_tab, e_blk, axis=0)
        ve = jnp.take(v_tab, e_blk, axis=0)
        act = jax.nn.gelu(jnp.einsum('td,tkd->tk', h_blk, ue))
        return jnp.einsum('tk,tkd->td', g_blk * act, ve)

    out = lax.map(one_block, (hb, experts, gates))
    return out.reshape(bsz, seq, dm)


def setup_inputs(seed: int = 0) -> dict:
    key = jax.random.key(seed)
    ks = jax.random.split(key, 32)
    f32 = jnp.float32

    def nrm(k, shape, scale):
        return jax.random.normal(k, shape, f32) * scale

    dt0 = jnp.exp(jax.random.uniform(ks[20], (N_ODD, SSD_HEADS), f32,
                                     minval=math.log(1e-3), maxval=math.log(1e-1)))
    return {
        'x': nrm(ks[0], (BATCH, SEQ, D_MODEL), 1.0),
        'c': nrm(ks[1], (BATCH, D_MODEL), 1.0),
        'ada_w1': nrm(ks[2], (D_MODEL, ADA_RANK), D_MODEL ** -0.5),
        'ada_w2': nrm(ks[3], (ADA_RANK, N_MOD * D_MODEL), 0.5 * ADA_RANK ** -0.5),
        'ada_b': nrm(ks[4], (N_MOD * D_MODEL,), 0.02),
        'ada_table': nrm(ks[5], (DEPTH, N_MOD * D_MODEL), 0.1),
        'norm_mix': 1.0 + nrm(ks[6], (DEPTH, D_MODEL), 0.05),
        'norm_ffn': 1.0 + nrm(ks[7], (DEPTH, D_MODEL), 0.05),
        'norm_final': 1.0 + nrm(ks[8], (D_MODEL,), 0.05),
        'hyb_w_in': nrm(ks[9], (N_EVEN, D_MODEL, HYB_IN), D_MODEL ** -0.5),
        'hyb_w_out': nrm(ks[10], (N_EVEN, HYB_OUT, D_MODEL), HYB_OUT ** -0.5),
        'diff_lam': nrm(ks[11], (N_EVEN, 4, ATT_HD), 0.1),
        'diff_subln': 1.0 + nrm(ks[12], (N_EVEN, ATT_VD), 0.05),
        'sgu_ln_g': 1.0 + nrm(ks[13], (N_EVEN, SGU_W), 0.05),
        'sgu_ln_b': nrm(ks[14], (N_EVEN, SGU_W), 0.02),
        'sgu_w_s': nrm(ks[15], (N_EVEN, SGU_GROUPS, SGU_CHUNK, SGU_CHUNK), SGU_CHUNK ** -0.5),
        'sgu_b_s': 1.0 + nrm(ks[16], (N_EVEN, SGU_GROUPS, SGU_CHUNK), 0.05),
        'ssd_w_in': nrm(ks[17], (N_ODD, D_MODEL, SSD_IN), D_MODEL ** -0.5),
        'ssd_conv_w': nrm(ks[18], (N_ODD, CONV_K, CONV_DIM), CONV_K ** -0.5),
        'ssd_conv_b': nrm(ks[19], (N_ODD, CONV_DIM), 0.02),
        'ssd_dt_bias': dt0 + jnp.log(-jnp.expm1(-dt0)),
        'ssd_a_log': jnp.log(jax.random.uniform(ks[21], (N_ODD, SSD_HEADS), f32, minval=1.0, maxval=16.0)),
        'ssd_d': 1.0 + nrm(ks[22], (N_ODD, SSD_HEADS), 0.1),
        'ssd_norm': 1.0 + nrm(ks[23], (N_ODD, D_INNER), 0.05),
        'ssd_w_out': nrm(ks[24], (N_ODD, D_INNER, D_MODEL), D_INNER ** -0.5),
        'peer_w_q': nrm(ks[25], (DEPTH, D_MODEL, PEER_HEADS * 2 * PEER_HALF), D_MODEL ** -0.5),
        'peer_keys': nrm(ks[26], (DEPTH, PEER_HEADS, 2, N_KEYS, PEER_HALF), PEER_HALF ** -0.5),
        'peer_u': nrm(ks[27], (DEPTH, N_EXPERTS, D_MODEL), D_MODEL ** -0.5),
        'peer_v': nrm(ks[28], (DEPTH, N_EXPERTS, D_MODEL), 0.5),
    }


def reference(x, c, ada_w1, ada_w2, ada_b, ada_table, norm_mix, norm_ffn, norm_final,
              hyb_w_in, hyb_w_out, diff_lam, diff_subln, sgu_ln_g, sgu_ln_b, sgu_w_s, sgu_b_s,
              ssd_w_in, ssd_conv_w, ssd_conv_b, ssd_dt_bias, ssd_a_log, ssd_d, ssd_norm, ssd_w_out,
              peer_w_q, peer_keys, peer_u, peer_v):
    t0 = (jax.nn.silu(c) @ ada_w1) @ ada_w2 + ada_b
    for l in range(DEPTH):
        mod = (t0 + ada_table[l])[:, None, :]
        sh1, sc1, g1, sh2, sc2, g2 = jnp.split(mod, N_MOD, axis=-1)
        h = _rmsnorm(x, norm_mix[l]) * (1 + sc1) + sh1
        if l % 2 == 0:
            e = l // 2
            lam_init = 0.8 - 0.6 * math.exp(-0.3 * l)
            y = _attn_sgu_mixer(h, hyb_w_in[e], hyb_w_out[e], diff_lam[e], diff_subln[e],
                                sgu_ln_g[e], sgu_ln_b[e], sgu_w_s[e], sgu_b_s[e], lam_init)
        else:
            o = l // 2
            y = _ssd_mixer(h, ssd_w_in[o], ssd_conv_w[o], ssd_conv_b[o], ssd_dt_bias[o],
                           ssd_a_log[o], ssd_d[o], ssd_norm[o], ssd_w_out[o])
        x = x + g1 * y
        h = _rmsnorm(x, norm_ffn[l]) * (1 + sc2) + sh2
        x = x + g2 * _peer(h, peer_w_q[l], peer_keys[l], peer_u[l], peer_v[l])
    return _rmsnorm(x, norm_final)
```

```python
import functools
import math

import jax
import jax.numpy as jnp
from jax import lax
from jax.experimental import pallas as pl
from jax.experimental.pallas import tpu as pltpu

F32 = jnp.float32
BF16 = jnp.bfloat16
NORM_EPS = 1e-6
LANES = 128
VMEM_LIMIT = 56 * 1024 * 1024
NEG = -1e30

ATT_HD = 64
SSD_HD = 64
SSD_GROUPS = 8
SSD_STATE = 128
CONV_K = 4
CHUNK = 128
PEER_HEADS = 8
N_KEYS = 128
PEER_TOPK = 16
ADA_N_MOD = 6


def _cparams(sem):
    return pltpu.CompilerParams(dimension_semantics=sem, vmem_limit_bytes=VMEM_LIMIT)


def _tile(pref, dim):
    t = min(pref, dim)
    while dim % t:
        t -= LANES
        assert t > 0, (pref, dim)
    return t


def _split2(a):
    hi = a.astype(BF16)
    lo = (a - hi.astype(F32)).astype(BF16)
    return hi, lo


def _split3(a):
    hi = a.astype(BF16)
    r = a - hi.astype(F32)
    mid = r.astype(BF16)
    lo = (r - mid.astype(F32)).astype(BF16)
    return hi, mid, lo


def _dot(a, b):
    return jnp.dot(a, b, preferred_element_type=F32)


def _dot_nt(a, b):
    return lax.dot_general(a, b, (((1,), (1,)), ((), ())), preferred_element_type=F32)


def _mm_body(*refs, nk, nt, n_extra, epilogue):
    a_ref, b_ref = refs[0], refs[1]
    extra = refs[2:2 + n_extra]
    o_ref = refs[2 + n_extra]
    part = _dot_nt(a_ref[...], b_ref[...]) if nt else _dot(a_ref[...], b_ref[...])
    if nk == 1:
        epilogue(part, o_ref, *extra)
        return
    acc_ref = refs[3 + n_extra]
    k = pl.program_id(2)

    @pl.when(k == 0)
    def _():
        acc_ref[...] = part

    @pl.when(k > 0)
    def _():
        acc_ref[...] += part

    @pl.when(k == nk - 1)
    def _():
        epilogue(acc_ref[...], o_ref, *extra)


def _store_epilogue(acc, o_ref):
    o_ref[...] = acc.astype(o_ref.dtype)


def _mm(a, b, *, nt=False, tm, tn, tk, out_shape, out_block=None, out_index=None,
        extras=(), epilogue=_store_epilogue, n_outer=False, name="mm"):
    m, kdim = a.shape
    n = b.shape[0] if nt else b.shape[1]
    tm, tn, tk = _tile(tm, m), _tile(tn, n), _tile(tk, kdim)
    nm, nn, nk = m // tm, n // tn, kdim // tk
    if n_outer:
        grid = (nn, nm, nk)
        ij = lambda g0, g1: (g1, g0)
    else:
        grid = (nm, nn, nk)
        ij = lambda g0, g1: (g0, g1)

    def wrap(fn):
        return lambda g0, g1, k: fn(*ij(g0, g1), k)

    a_spec = pl.BlockSpec((tm, tk), wrap(lambda i, j, k: (i, k)))
    if nt:
        b_spec = pl.BlockSpec((tn, tk), wrap(lambda i, j, k: (j, k)))
    else:
        b_spec = pl.BlockSpec((tk, tn), wrap(lambda i, j, k: (k, j)))
    extra_specs = [pl.BlockSpec(blk, wrap(lambda i, j, k, f=f: f(i, j))) for _, blk, f in extras]
    if out_block is None:
        out_block, out_index = (tm, tn), (lambda i, j: (i, j))
    o_spec = pl.BlockSpec(out_block, wrap(lambda i, j, k: out_index(i, j)))
    scratch = [pltpu.VMEM((tm, tn), F32)] if nk > 1 else []
    body = functools.partial(_mm_body, nk=nk, nt=nt, n_extra=len(extras), epilogue=epilogue)
    return pl.pallas_call(
        body, out_shape=out_shape, grid=grid,
        in_specs=[a_spec, b_spec] + extra_specs, out_specs=o_spec,
        scratch_shapes=scratch, name=name,
        compiler_params=_cparams(("parallel", "parallel", "arbitrary")),
    )(a, b, *[e[0] for e in extras])


def _resid_epilogue(acc, o_ref, x_ref, g_ref):
    o_ref[...] = x_ref[...] + g_ref[0] * acc


def _resid_t_epilogue(acc, o_ref, x_ref, g_ref):
    o_ref[...] = x_ref[...] + g_ref[0] * acc.T


def _dot_split(a, b):
    a_hi, a_lo = _split2(a)
    b_hi, b_lo = _split2(b)
    return _dot(a_hi, b_hi) + _dot(a_hi, b_lo) + _dot(a_lo, b_hi)


def _ada_body(c_ref, w1_ref, w2_ref, b_ref, tab_ref, o_ref):
    c = c_ref[...]
    t = c * jax.nn.sigmoid(c)
    t1 = _dot_split(t, w1_ref[...])
    t0 = _dot_split(t1, w2_ref[...]) + b_ref[...]
    for l in range(o_ref.shape[0]):
        o_ref[l] = t0 + tab_ref[l:l + 1, :]


def _ada(c, w1, w2, b, table):
    bsz, d = c.shape
    depth, n = table.shape
    rank = w1.shape[1]
    bp = 8 * ((bsz + 7) // 8)
    cp = jnp.zeros((bp, d), F32).at[:bsz].set(c)
    tn = _tile(2048, d)
    out = pl.pallas_call(
        _ada_body, out_shape=jax.ShapeDtypeStruct((depth, bp, n), F32), grid=(n // tn,),
        in_specs=[pl.BlockSpec((bp, d), lambda j: (0, 0)),
                  pl.BlockSpec((d, rank), lambda j: (0, 0)),
                  pl.BlockSpec((rank, tn), lambda j: (0, j)),
                  pl.BlockSpec((1, tn), lambda j: (0, j)),
                  pl.BlockSpec((depth, tn), lambda j: (0, j))],
        out_specs=pl.BlockSpec((depth, bp, tn), lambda j: (0, 0, j)),
        name="ada", compiler_params=_cparams(("parallel",)),
    )(cp, w1, w2, b.reshape(1, n), table)
    return out[:, :bsz].reshape(depth * bsz, 1, n)


def _rms(x, g):
    return x * lax.rsqrt(jnp.mean(x * x, axis=-1, keepdims=True) + NORM_EPS) * g


def _norm_mod_body(x_ref, g_ref, sc_ref, sh_ref, o_ref):
    y = _rms(x_ref[...], g_ref[...])
    o_ref[...] = (y * (1.0 + sc_ref[0]) + sh_ref[0]).astype(o_ref.dtype)


def _norm_body(x_ref, g_ref, o_ref):
    o_ref[...] = _rms(x_ref[...], g_ref[...]).astype(o_ref.dtype)


def _norm_mod(x, g, mod, row0, sh_col, sc_col, seq):
    t, d = x.shape
    tm = _tile(256, seq)
    return pl.pallas_call(
        _norm_mod_body, out_shape=jax.ShapeDtypeStruct((t, d), BF16), grid=(t // tm,),
        in_specs=[pl.BlockSpec((tm, d), lambda i: (i, 0)),
                  pl.BlockSpec((1, d), lambda i: (0, 0)),
                  pl.BlockSpec((1, 1, d), lambda i: (row0 + (i * tm) // seq, 0, sc_col)),
                  pl.BlockSpec((1, 1, d), lambda i: (row0 + (i * tm) // seq, 0, sh_col))],
        out_specs=pl.BlockSpec((tm, d), lambda i: (i, 0)),
        name="norm_mod", compiler_params=_cparams(("parallel",)),
    )(x, g.reshape(1, d), mod, mod)


def _final_norm(x, g):
    t, d = x.shape
    tm = _tile(256, t)
    return pl.pallas_call(
        _norm_body, out_shape=jax.ShapeDtypeStruct((t, d), F32), grid=(t // tm,),
        in_specs=[pl.BlockSpec((tm, d), lambda i: (i, 0)),
                  pl.BlockSpec((1, d), lambda i: (0, 0))],
        out_specs=pl.BlockSpec((tm, d), lambda i: (i, 0)),
        name="final_norm", compiler_params=_cparams(("parallel",)),
    )(x, g.reshape(1, d))


def _attn_body(q_ref, k_ref, v_ref, lam_ref, g_ref, o_ref,
               q1_s, q2_s, m1_s, l1_s, a1_s, m2_s, l2_s, a2_s, *, tq, tk, nkv, lam_init):
    i = pl.program_id(2)
    j = pl.program_id(3)

    @pl.when(j == 0)
    def _():
        q = q_ref[...] * (ATT_HD ** -0.5)
        lane = lax.broadcasted_iota(jnp.int32, q.shape, 1)
        q1_s[...] = jnp.where(lane < ATT_HD, q, 0).astype(BF16)
        q2_s[...] = jnp.where(lane >= ATT_HD, q, 0).astype(BF16)
        for m_s, l_s, a_s in ((m1_s, l1_s, a1_s), (m2_s, l2_s, a2_s)):
            m_s[...] = jnp.full(m_s.shape, NEG, F32)
            l_s[...] = jnp.zeros(l_s.shape, F32)
            a_s[...] = jnp.zeros(a_s.shape, F32)

    @pl.when(j * tk <= i * tq + tq - 1)
    def _():
        k = k_ref[...]
        v = v_ref[...]
        row = i * tq + lax.broadcasted_iota(jnp.int32, (tq, tk), 0)
        col = j * tk + lax.broadcasted_iota(jnp.int32, (tq, tk), 1)
        causal = col <= row
        for q_s, m_s, l_s, a_s in ((q1_s, m1_s, l1_s, a1_s), (q2_s, m2_s, l2_s, a2_s)):
            s = jnp.where(causal, _dot_nt(q_s[...], k), NEG)
            m_old = m_s[...]
            m_new = jnp.maximum(m_old, jnp.max(s, axis=-1, keepdims=True))
            alpha = jnp.exp(m_old - m_new)
            p = jnp.exp(s - m_new)
            l_s[...] = alpha * l_s[...] + jnp.sum(p, axis=-1, keepdims=True)
            a_s[...] = alpha * a_s[...] + _dot(p.astype(BF16), v)
            m_s[...] = m_new

    @pl.when(j == nkv - 1)
    def _():
        lp = lam_ref[...]
        lam = (jnp.exp(jnp.sum(lp[0:1] * lp[1:2], axis=-1, keepdims=True))
               - jnp.exp(jnp.sum(lp[2:3] * lp[3:4], axis=-1, keepdims=True)) + lam_init)
        o = a1_s[...] / l1_s[...] - lam * (a2_s[...] / l2_s[...])
        o_ref[...] = (_rms(o, g_ref[...]) * (1.0 - lam_init)).astype(o_ref.dtype)


def _diff_attention(qkv, lam_p, subln_g, lam_init, bsz, seq, heads):
    t = qkv.shape[0]
    vd = 2 * ATT_HD
    tq = tk = _tile(512, seq)
    nq, nkv = seq // tq, seq // tk
    last = lambda i: ((i + 1) * tq - 1) // tk
    body = functools.partial(_attn_body, tq=tq, tk=tk, nkv=nkv, lam_init=lam_init)
    return pl.pallas_call(
        body, out_shape=jax.ShapeDtypeStruct((t, heads * vd), BF16),
        grid=(bsz, heads, nq, nkv),
        in_specs=[pl.BlockSpec((tq, vd), lambda b, h, i, j: (b * nq + i, h)),
                  pl.BlockSpec((tk, vd), lambda b, h, i, j: (b * nkv + jnp.minimum(j, last(i)), heads + h)),
                  pl.BlockSpec((tk, vd), lambda b, h, i, j: (b * nkv + jnp.minimum(j, last(i)), 2 * heads + h)),
                  pl.BlockSpec((4, ATT_HD), lambda b, h, i, j: (0, 0)),
                  pl.BlockSpec((1, vd), lambda b, h, i, j: (0, 0))],
        out_specs=pl.BlockSpec((tq, vd), lambda b, h, i, j: (b * nq + i, h)),
        scratch_shapes=[pltpu.VMEM((tq, vd), BF16), pltpu.VMEM((tq, vd), BF16),
                        pltpu.VMEM((tq, 1), F32), pltpu.VMEM((tq, 1), F32), pltpu.VMEM((tq, vd), F32),
                        pltpu.VMEM((tq, 1), F32), pltpu.VMEM((tq, 1), F32), pltpu.VMEM((tq, vd), F32)],
        name="diff_attn",
        compiler_params=_cparams(("parallel", "parallel", "parallel", "arbitrary")),
    )(qkv, qkv, qkv, lam_p, subln_g.reshape(1, vd))


def _sgu_body(u_ref, v_ref, lng_ref, lnb_ref, w_ref, bias_ref, o_ref, *, groups, nchunk):
    u = jax.nn.gelu(u_ref[...])
    v = jax.nn.gelu(v_ref[...])
    mu = jnp.mean(v, axis=-1, keepdims=True)
    vc = v - mu
    var = jnp.mean(vc * vc, axis=-1, keepdims=True)
    vn = (vc * lax.rsqrt(var + NORM_EPS) * lng_ref[...] + lnb_ref[...]).astype(BF16)
    ri = lax.broadcasted_iota(jnp.int32, (CHUNK, CHUNK), 0)
    ci = lax.broadcasted_iota(jnp.int32, (CHUNK, CHUNK), 1)
    causal = ci <= ri
    for g in range(groups):
        cs = slice(g * LANES, (g + 1) * LANES)
        wg = jnp.where(causal, w_ref[g], 0.0).astype(BF16)
        bias = bias_ref[:, cs]
        for c in range(nchunk):
            rs = slice(c * CHUNK, (c + 1) * CHUNK)
            mixed = _dot(wg, vn[rs, cs]) + bias
            o_ref[rs, cs] = (u[rs, cs] * mixed).astype(o_ref.dtype)


def _sgu(ug, ln_g, ln_b, w_s, b_s):
    t, w2 = ug.shape
    w = w2 // 2
    groups = w // LANES
    ts = _tile(256, t)
    bias = jnp.repeat(b_s.T, LANES, axis=1)
    body = functools.partial(_sgu_body, groups=groups, nchunk=ts // CHUNK)
    return pl.pallas_call(
        body, out_shape=jax.ShapeDtypeStruct((t, w), BF16), grid=(t // ts,),
        in_specs=[pl.BlockSpec((ts, w), lambda i: (i, 0)),
                  pl.BlockSpec((ts, w), lambda i: (i, 1)),
                  pl.BlockSpec((1, w), lambda i: (0, 0)),
                  pl.BlockSpec((1, w), lambda i: (0, 0)),
                  pl.BlockSpec((groups, CHUNK, CHUNK), lambda i: (0, 0, 0)),
                  pl.BlockSpec((CHUNK, w), lambda i: (0, 0))],
        out_specs=pl.BlockSpec((ts, w), lambda i: (i, 0)),
        name="sgu", compiler_params=_cparams(("parallel",)),
    )(ug, ug, ln_g.reshape(1, w), ln_b.reshape(1, w), w_s, bias)


def _conv_body(x_ref, w_ref, b_ref, o_ref, prev_s):
    @pl.when(pl.program_id(2) == 0)
    def _():
        prev_s[...] = jnp.zeros(prev_s.shape, F32)

    x = x_ref[...]
    prev = prev_s[...]
    row = lax.broadcasted_iota(jnp.int32, x.shape, 0)
    acc = x * w_ref[CONV_K - 1:CONV_K, :] + b_ref[...]
    for k in range(1, CONV_K):
        xs = jnp.where(row < k, pltpu.roll(prev, k, 0), pltpu.roll(x, k, 0))
        acc = acc + xs * w_ref[CONV_K - 1 - k:CONV_K - k, :]
    o_ref[...] = acc * jax.nn.sigmoid(acc)
    prev_s[...] = x


def _conv_silu(xbc, w, b, bsz, seq):
    t, c = xbc.shape
    ts = _tile(256, seq)
    tc = _tile(2048, c)
    ns = seq // ts
    return pl.pallas_call(
        _conv_body, out_shape=jax.ShapeDtypeStruct((t, c), F32), grid=(c // tc, bsz, ns),
        in_specs=[pl.BlockSpec((ts, tc), lambda ci, bi, si: (bi * ns + si, ci)),
                  pl.BlockSpec((CONV_K, tc), lambda ci, bi, si: (0, ci)),
                  pl.BlockSpec((1, tc), lambda ci, bi, si: (0, ci))],
        out_specs=pl.BlockSpec((ts, tc), lambda ci, bi, si: (bi * ns + si, ci)),
        scratch_shapes=[pltpu.VMEM((ts, tc), F32)],
        name="conv_silu", compiler_params=_cparams(("parallel", "parallel", "arbitrary")),
    )(xbc, w, b.reshape(1, c))


def _softplus(x):
    return jnp.maximum(x, 0.0) + jnp.log1p(jnp.exp(-jnp.abs(x)))


def _ssd_body(x_ref, b_ref, c_ref, z_ref, dta_ref, dtb_ref, ba_ref, bb_ref, ala_ref, alb_ref,
              d_ref, ng_ref, o_ref, state_s, y_s, *, hpg):
    n = pl.program_id(2)
    L = CHUNK
    gw = hpg * SSD_HD

    @pl.when(n == 0)
    def _():
        state_s[...] = jnp.zeros(state_s.shape, F32)

    dt = _softplus(dta_ref[0] + ba_ref[0])
    dt_t = _softplus(dtb_ref[0] + bb_ref[0])
    da = dt * (-jnp.exp(ala_ref[0]))
    da_t = dt_t * (-jnp.exp(alb_ref[0]))
    ri = lax.broadcasted_iota(jnp.int32, (L, L), 0)
    ci = lax.broadcasted_iota(jnp.int32, (L, L), 1)
    causal = ci <= ri
    tri = causal.astype(BF16)
    tri_t = (ri <= ci).astype(BF16)
    acs = sum(_dot(tri, p) for p in _split3(da))
    acs_t = sum(_dot(p, tri_t) for p in _split3(da_t))
    acs_last = acs[L - 1:L, :]

    head_of_col = lax.broadcasted_iota(jnp.int32, (hpg, gw), 1) // SSD_HD
    expand = (head_of_col == lax.broadcasted_iota(jnp.int32, (hpg, gw), 0)).astype(BF16)

    def widen(v):
        return sum(_dot(p, expand) for p in _split2(v))

    dt_e = widen(dt)
    dec_e = widen(jnp.exp(acs))
    tail_e = widen(jnp.exp(acs_last - acs))

    x = x_ref[...]
    xdt = x * dt_e
    xdt_b = xdt.astype(BF16)
    bm = b_ref[...]
    cm_b = c_ref[...].astype(BF16)
    bm_b = bm.astype(BF16)
    cb = _dot_nt(cm_b, bm_b)
    state = state_s[...]
    y_s[...] = _dot(cm_b, state.astype(BF16)) * dec_e

    lane = lax.broadcasted_iota(jnp.int32, (L, LANES), 1)
    for pair in range(hpg // 2):
        cs = slice(pair * LANES, (pair + 1) * LANES)
        xp = xdt_b[:, cs]
        acc = None
        for half in range(2):
            r = 2 * pair + half
            seg = acs[:, r:r + 1] - acs_t[r:r + 1, :]
            decay = jnp.exp(jnp.where(causal, seg, NEG))
            mm = (cb * decay).astype(BF16)
            keep = (lane < SSD_HD) if half == 0 else (lane >= SSD_HD)
            part = _dot(mm, jnp.where(keep, xp, 0))
            acc = part if acc is None else acc + part
        y_s[:, cs] += acc

    state_s[...] = state * dec_e[L - 1:L, :] + _dot(bm.T.astype(BF16), (tail_e * xdt).astype(BF16))

    y = y_s[...] + x * d_ref[...]
    z = z_ref[...].astype(F32)
    y = y * (z * jax.nn.sigmoid(z))
    o_ref[...] = _rms(y, ng_ref[...]).astype(o_ref.dtype)


def _ssd_scan(xc, z, dt_raw, dt_bias, a_log, d_skip, norm_g, bsz, seq, d_inner):
    t = xc.shape[0]
    g, n_state = SSD_GROUPS, SSD_STATE
    heads = dt_raw.shape[1]
    hpg = heads // g
    gw = hpg * SSD_HD
    assert gw % LANES == 0 and hpg % 2 == 0
    nc = seq // CHUNK
    dta = dt_raw.reshape(t, g, hpg).transpose(1, 0, 2)
    dtb = dta.transpose(0, 2, 1)
    ba = dt_bias.reshape(g, 1, hpg)
    bb = dt_bias.reshape(g, hpg, 1)
    ala = a_log.reshape(g, 1, hpg)
    alb = a_log.reshape(g, hpg, 1)
    d_e = jnp.repeat(d_skip, SSD_HD).reshape(1, d_inner)
    xcol = gw // LANES
    b0 = d_inner // n_state
    c0 = (d_inner + g * n_state) // n_state
    row = lambda bi, ni: bi * nc + ni
    body = functools.partial(_ssd_body, hpg=hpg)
    return pl.pallas_call(
        body, out_shape=jax.ShapeDtypeStruct((t, d_inner), BF16), grid=(bsz, g, nc),
        in_specs=[pl.BlockSpec((CHUNK, gw), lambda bi, gi, ni: (row(bi, ni), gi)),
                  pl.BlockSpec((CHUNK, n_state), lambda bi, gi, ni: (row(bi, ni), b0 + gi)),
                  pl.BlockSpec((CHUNK, n_state), lambda bi, gi, ni: (row(bi, ni), c0 + gi)),
                  pl.BlockSpec((CHUNK, gw), lambda bi, gi, ni: (row(bi, ni), gi)),
                  pl.BlockSpec((1, CHUNK, hpg), lambda bi, gi, ni: (gi, row(bi, ni), 0)),
                  pl.BlockSpec((1, hpg, CHUNK), lambda bi, gi, ni: (gi, 0, row(bi, ni))),
                  pl.BlockSpec((1, 1, hpg), lambda bi, gi, ni: (gi, 0, 0)),
                  pl.BlockSpec((1, hpg, 1), lambda bi, gi, ni: (gi, 0, 0)),
                  pl.BlockSpec((1, 1, hpg), lambda bi, gi, ni: (gi, 0, 0)),
                  pl.BlockSpec((1, hpg, 1), lambda bi, gi, ni: (gi, 0, 0)),
                  pl.BlockSpec((1, gw), lambda bi, gi, ni: (0, gi)),
                  pl.BlockSpec((1, gw), lambda bi, gi, ni: (0, gi))],
        out_specs=pl.BlockSpec((CHUNK, gw), lambda bi, gi, ni: (row(bi, ni), gi)),
        scratch_shapes=[pltpu.VMEM((n_state, gw), F32), pltpu.VMEM((CHUNK, gw), F32)],
        name="ssd_scan", compiler_params=_cparams(("parallel", "parallel", "arbitrary")),
    )(xc, xc, xc, z, dta, dtb, ba, bb, ala, alb, d_e, norm_g.reshape(1, d_inner))


def _topk_rank(s, k_top):
    n = s.shape[0]
    iota = lax.broadcasted_iota(jnp.int32, s.shape, 0).astype(F32)
    rank = jnp.full(s.shape, float(k_top), F32)
    vals = []
    for k in range(k_top):
        m = jnp.max(s, axis=0, keepdims=True)
        idx = jnp.min(jnp.where(s == m, iota, float(n)), axis=0, keepdims=True)
        hit = iota == idx
        rank = jnp.where(hit, float(k), rank)
        s = jnp.where(hit, -jnp.inf, s)
        vals.append(m)
    return rank, jnp.concatenate(vals, axis=0)


def _route_body(q_ref, k_ref, lk_ref, r2_ref, e1_ref, e2_ref):
    kt = PEER_TOPK
    q = q_ref[...]
    s1 = _dot_nt(k_ref[0, 0], q[:, :N_KEYS])
    s2 = _dot_nt(k_ref[0, 1], q[:, N_KEYS:])
    r1, v1 = _topk_rank(s1, kt)
    r2, v2 = _topk_rank(s2, kt)
    cand = jnp.concatenate([v1[a:a + 1] + v2 for a in range(kt)], axis=0)
    crank, _ = _topk_rank(cand, kt)
    sel = (crank < float(kt)).astype(F32)
    e1 = jnp.exp(v1 - v1[0:1])
    e2 = jnp.exp(v2 - v2[0:1])
    lk = jnp.zeros(s1.shape, F32)
    zsum = jnp.zeros((1, s1.shape[1]), F32)
    for a in range(kt):
        sa = sel[a * kt:(a + 1) * kt]
        count = jnp.sum(sa, axis=0, keepdims=True)
        zsum = zsum + e1[a:a + 1] * jnp.sum(sa * e2, axis=0, keepdims=True)
        lk = jnp.where(r1 == float(a), count, lk)
    lk_ref[0] = lk
    r2_ref[0] = r2
    e1_ref[0] = jnp.exp(s1 - v1[0:1])
    e2_ref[0] = jnp.exp(s2 - v2[0:1]) / zsum


def _peer_route(q, keys):
    t = q.shape[0]
    tt = _tile(256, t)
    shp = jax.ShapeDtypeStruct((PEER_HEADS, N_KEYS, t), F32)
    ospec = pl.BlockSpec((1, N_KEYS, tt), lambda i, h: (h, 0, i))
    return pl.pallas_call(
        _route_body, out_shape=(shp, shp, shp, shp), grid=(t // tt, PEER_HEADS),
        in_specs=[pl.BlockSpec((tt, 2 * N_KEYS), lambda i, h: (i, h)),
                  pl.BlockSpec((1, 2, N_KEYS, N_KEYS), lambda i, h: (h, 0, 0, 0))],
        out_specs=(ospec, ospec, ospec, ospec),
        name="peer_route", compiler_params=_cparams(("parallel", "parallel")),
    )(q, keys)


def _peer_gate_epilogue(acc, o_ref, lk_ref, e1_ref, r2_ref, e2_ref):
    for ii in range(acc.shape[0] // N_KEYS):
        rs = slice(ii * N_KEYS, (ii + 1) * N_KEYS)
        gate = None
        for h in range(PEER_HEADS):
            lk = lk_ref[h, ii:ii + 1, :]
            e1 = e1_ref[h, ii:ii + 1, :]
            term = e1 * jnp.where(r2_ref[h] < lk, e2_ref[h], 0.0)
            gate = term if gate is None else gate + term
        o_ref[rs, :] = (jax.nn.gelu(acc[rs, :]) * gate).astype(o_ref.dtype)


def kernel(x, c, ada_w1, ada_w2, ada_b, ada_table, norm_mix, norm_ffn, norm_final, hyb_w_in, hyb_w_out, diff_lam, diff_subln, sgu_ln_g, sgu_ln_b, sgu_w_s, sgu_b_s, ssd_w_in, ssd_conv_w, ssd_conv_b, ssd_dt_bias, ssd_a_log, ssd_d, ssd_norm, ssd_w_out, peer_w_q, peer_keys, peer_u, peer_v):
    bsz, seq, d = x.shape
    depth = ada_table.shape[0]
    t = bsz * seq
    n_experts = peer_u.shape[1]
    d_inner = ssd_norm.shape[1]
    ssd_heads = ssd_a_log.shape[1]
    conv_dim = ssd_conv_b.shape[1]
    sgu_w = sgu_ln_g.shape[1]
    att_w = hyb_w_out.shape[1] - sgu_w
    att_heads = att_w // (2 * ATT_HD)

    mod = _ada(c, ada_w1, ada_w2, ada_b, ada_table)
    xt = x.reshape(t, d)

    def gate_spec(l, col, tn):
        per = d // tn
        return lambda tm: (mod, (1, 1, tn), lambda i, j: (l * bsz + (i * tm) // seq, 0, col * per + j))

    for l in range(depth):
        h = _norm_mod(xt, norm_mix[l], mod, l * bsz, 0, 1, seq)
        if l % 2 == 0:
            e = l // 2
            lam_init = 0.8 - 0.6 * math.exp(-0.3 * l)
            w_in = hyb_w_in[e]
            qkv = _mm(h, w_in[:, :3 * att_w].astype(BF16), tm=1024, tn=1024, tk=d,
                      out_shape=jax.ShapeDtypeStruct((t, 3 * att_w), BF16), name="hyb_qkv")
            ug = _mm(h, w_in[:, 3 * att_w:].astype(BF16), tm=1024, tn=1024, tk=d,
                     out_shape=jax.ShapeDtypeStruct((t, 2 * sgu_w), F32), name="hyb_ug")
            a_out = _diff_attention(qkv, diff_lam[e], diff_subln[e], lam_init, bsz, seq, att_heads)
            s_out = _sgu(ug, sgu_ln_g[e], sgu_ln_b[e], sgu_w_s[e], sgu_b_s[e])
            y_in = jnp.concatenate([a_out, s_out], axis=-1)
            w_out = hyb_w_out[e].astype(BF16)
        else:
            o = l // 2
            w_in = ssd_w_in[o]
            z = _mm(h, w_in[:, :d_inner].astype(BF16), tm=1024, tn=1024, tk=d,
                    out_shape=jax.ShapeDtypeStruct((t, d_inner), F32), name="ssd_z")
            xbc = _mm(h, w_in[:, d_inner:d_inner + conv_dim].astype(BF16), tm=1024, tn=1024, tk=d,
                      out_shape=jax.ShapeDtypeStruct((t, conv_dim), F32), name="ssd_xbc")
            dt_raw = _mm(h, w_in[:, d_inner + conv_dim:].astype(BF16), tm=1024, tn=ssd_heads, tk=d,
                         out_shape=jax.ShapeDtypeStruct((t, ssd_heads), F32), name="ssd_dt")
            xc = _conv_silu(xbc, ssd_conv_w[o], ssd_conv_b[o], bsz, seq)
            y_in = _ssd_scan(xc, z, dt_raw, ssd_dt_bias[o], ssd_a_log[o], ssd_d[o], ssd_norm[o],
                             bsz, seq, d_inner)
            w_out = ssd_w_out[o].astype(BF16)
        tm, tn = _tile(1024, seq), _tile(1024, d)
        xt = _mm(y_in, w_out, tm=tm, tn=tn, tk=2048,
                 out_shape=jax.ShapeDtypeStruct((t, d), F32),
                 extras=[(xt, (tm, tn), lambda i, j: (i, j)), gate_spec(l, 2, tn)(tm)],
                 epilogue=_resid_epilogue, name="mix_out")

        h = _norm_mod(xt, norm_ffn[l], mod, l * bsz, 3, 4, seq)
        q = _mm(h, peer_w_q[l].astype(BF16), tm=1024, tn=1024, tk=d,
                out_shape=jax.ShapeDtypeStruct((t, peer_w_q.shape[2]), BF16), name="peer_q")
        lk, r2, e1, e2 = _peer_route(q, peer_keys[l].astype(BF16))
        te, tt = _tile(1024, n_experts), _tile(512, t)
        ni1 = te // N_KEYS
        head_blk = (PEER_HEADS, ni1, tt)
        full_blk = (PEER_HEADS, N_KEYS, tt)
        wt = _mm(peer_u[l].astype(BF16), h, nt=True, tm=te, tn=tt, tk=d, n_outer=True,
                 out_shape=jax.ShapeDtypeStruct((n_experts, t), BF16),
                 extras=[(lk, head_blk, lambda i, j: (0, i, j)),
                         (e1, head_blk, lambda i, j: (0, i, j)),
                         (r2, full_blk, lambda i, j: (0, 0, j)),
                         (e2, full_blk, lambda i, j: (0, 0, j))],
                 epilogue=_peer_gate_epilogue, name="peer_gate")
        td, tt = _tile(1024, d), _tile(1024, seq)
        xt = _mm(peer_v[l].astype(BF16).T, wt, tm=td, tn=tt, tk=1024,
                 out_shape=jax.ShapeDtypeStruct((t, d), F32),
                 out_block=(tt, td), out_index=lambda i, j: (j, i),
                 extras=[(xt, (tt, td), lambda i, j: (j, i)),
                         (mod, (1, 1, td), lambda i, j, l=l, tt=tt, td=td:
                          (l * bsz + (j * tt) // seq, 0, 5 * (d // td) + i))],
                 epilogue=_resid_t_epilogue, name="peer_out")

    return _final_norm(xt, norm_final).reshape(bsz, seq, d)
```

```python
import functools
import math

import jax
import jax.numpy as jnp
import numpy as np
from jax import lax
from jax.experimental import pallas as pl
from jax.experimental.pallas import tpu as pltpu

F32 = jnp.float32
BF16 = jnp.bfloat16
NORM_EPS = 1e-6
LANES = 128
VMEM_LIMIT = 56 * 1024 * 1024
NEG = -1e30

ATT_HD = 64
SSD_HD = 64
SSD_GROUPS = 8
SSD_STATE = 128
CONV_K = 4
CHUNK = 128
PEER_HEADS = 8
N_KEYS = 128
PEER_TOPK = 16
ADA_N_MOD = 6


def _cparams(sem):
    return pltpu.CompilerParams(dimension_semantics=sem, vmem_limit_bytes=VMEM_LIMIT)


def _tile(pref, dim):
    t = min(pref, dim)
    while dim % t:
        t -= LANES
        assert t > 0, (pref, dim)
    return t


def _split2(a):
    hi = a.astype(BF16)
    lo = (a - hi.astype(F32)).astype(BF16)
    return hi, lo


def _split3(a):
    hi = a.astype(BF16)
    r = a - hi.astype(F32)
    mid = r.astype(BF16)
    lo = (r - mid.astype(F32)).astype(BF16)
    return hi, mid, lo


def _dot(a, b):
    return jnp.dot(a, b, preferred_element_type=F32)


def _dot_nt(a, b):
    return lax.dot_general(a, b, (((1,), (1,)), ((), ())), preferred_element_type=F32)


def _mm_body(*refs, nk, nt, n_extra, epilogue):
    a_ref, b_ref = refs[0], refs[1]
    extra = refs[2:2 + n_extra]
    o_ref = refs[2 + n_extra]
    part = _dot_nt(a_ref[...], b_ref[...]) if nt else _dot(a_ref[...], b_ref[...])
    if nk == 1:
        epilogue(part, o_ref, *extra)
        return
    acc_ref = refs[3 + n_extra]
    k = pl.program_id(2)

    @pl.when(k == 0)
    def _():
        acc_ref[...] = part

    @pl.when(k > 0)
    def _():
        acc_ref[...] += part

    @pl.when(k == nk - 1)
    def _():
        epilogue(acc_ref[...], o_ref, *extra)


def _store_epilogue(acc, o_ref):
    o_ref[...] = acc.astype(o_ref.dtype)


def _mm(a, b, *, nt=False, tm, tn, tk, out_shape, out_block=None, out_index=None,
        extras=(), epilogue=_store_epilogue, n_outer=False, name="mm"):
    m, kdim = a.shape
    n = b.shape[0] if nt else b.shape[1]
    tm, tn, tk = _tile(tm, m), _tile(tn, n), _tile(tk, kdim)
    nm, nn, nk = m // tm, n // tn, kdim // tk
    if n_outer:
        grid = (nn, nm, nk)
        ij = lambda g0, g1: (g1, g0)
    else:
        grid = (nm, nn, nk)
        ij = lambda g0, g1: (g0, g1)

    def wrap(fn):
        return lambda g0, g1, k: fn(*ij(g0, g1), k)

    a_spec = pl.BlockSpec((tm, tk), wrap(lambda i, j, k: (i, k)))
    if nt:
        b_spec = pl.BlockSpec((tn, tk), wrap(lambda i, j, k: (j, k)))
    else:
        b_spec = pl.BlockSpec((tk, tn), wrap(lambda i, j, k: (k, j)))
    extra_specs = [pl.BlockSpec(blk, wrap(lambda i, j, k, f=f: f(i, j))) for _, blk, f in extras]
    if out_block is None:
        out_block, out_index = (tm, tn), (lambda i, j: (i, j))
    o_spec = pl.BlockSpec(out_block, wrap(lambda i, j, k: out_index(i, j)))
    scratch = [pltpu.VMEM((tm, tn), F32)] if nk > 1 else []
    body = functools.partial(_mm_body, nk=nk, nt=nt, n_extra=len(extras), epilogue=epilogue)
    return pl.pallas_call(
        body, out_shape=out_shape, grid=grid,
        in_specs=[a_spec, b_spec] + extra_specs, out_specs=o_spec,
        scratch_shapes=scratch, name=name,
        compiler_params=_cparams(("parallel", "parallel", "arbitrary")),
    )(a, b, *[e[0] for e in extras])


def _resid_epilogue(acc, o_ref, x_ref, g_ref):
    o_ref[...] = x_ref[...] + g_ref[0] * acc


def _resid_t_epilogue(acc, o_ref, x_ref, g_ref):
    o_ref[...] = x_ref[...] + g_ref[0] * acc.T


def _dot_split(a, b):
    a_hi, a_lo = _split2(a)
    b_hi, b_lo = _split2(b)
    return _dot(a_hi, b_hi) + _dot(a_hi, b_lo) + _dot(a_lo, b_hi)


def _ada_body(c_ref, w1_ref, w2_ref, b_ref, tab_ref, o_ref):
    c = c_ref[...]
    t = c * jax.nn.sigmoid(c)
    t1 = _dot_split(t, w1_ref[...])
    t0 = _dot_split(t1, w2_ref[...]) + b_ref[...]
    for l in range(o_ref.shape[0]):
        o_ref[l] = t0 + tab_ref[l:l + 1, :]


def _ada(c, w1, w2, b, table):
    bsz, d = c.shape
    depth, n = table.shape
    rank = w1.shape[1]
    bp = 8 * ((bsz + 7) // 8)
    cp = jnp.zeros((bp, d), F32).at[:bsz].set(c)
    tn = _tile(2048, d)
    out = pl.pallas_call(
        _ada_body, out_shape=jax.ShapeDtypeStruct((depth, bp, n), F32), grid=(n // tn,),
        in_specs=[pl.BlockSpec((bp, d), lambda j: (0, 0)),
                  pl.BlockSpec((d, rank), lambda j: (0, 0)),
                  pl.BlockSpec((rank, tn), lambda j: (0, j)),
                  pl.BlockSpec((1, tn), lambda j: (0, j)),
                  pl.BlockSpec((depth, tn), lambda j: (0, j))],
        out_specs=pl.BlockSpec((depth, bp, tn), lambda j: (0, 0, j)),
        name="ada", compiler_params=_cparams(("parallel",)),
    )(cp, w1, w2, b.reshape(1, n), table)
    return out[:, :bsz].reshape(depth * bsz, 1, n)


def _rms(x, g):
    return x * lax.rsqrt(jnp.mean(x * x, axis=-1, keepdims=True) + NORM_EPS) * g


def _norm_mod_body(x_ref, g_ref, sc_ref, sh_ref, o_ref):
    y = _rms(x_ref[...], g_ref[...])
    o_ref[...] = (y * (1.0 + sc_ref[0]) + sh_ref[0]).astype(o_ref.dtype)


def _norm_body(x_ref, g_ref, o_ref):
    o_ref[...] = _rms(x_ref[...], g_ref[...]).astype(o_ref.dtype)


def _norm_mod(x, g, mod, row0, sh_col, sc_col, seq):
    t, d = x.shape
    tm = _tile(256, seq)
    return pl.pallas_call(
        _norm_mod_body, out_shape=jax.ShapeDtypeStruct((t, d), BF16), grid=(t // tm,),
        in_specs=[pl.BlockSpec((tm, d), lambda i: (i, 0)),
                  pl.BlockSpec((1, d), lambda i: (0, 0)),
                  pl.BlockSpec((1, 1, d), lambda i: (row0 + (i * tm) // seq, 0, sc_col)),
                  pl.BlockSpec((1, 1, d), lambda i: (row0 + (i * tm) // seq, 0, sh_col))],
        out_specs=pl.BlockSpec((tm, d), lambda i: (i, 0)),
        name="norm_mod", compiler_params=_cparams(("parallel",)),
    )(x, g.reshape(1, d), mod, mod)


def _final_norm(x, g):
    t, d = x.shape
    tm = _tile(256, t)
    return pl.pallas_call(
        _norm_body, out_shape=jax.ShapeDtypeStruct((t, d), F32), grid=(t // tm,),
        in_specs=[pl.BlockSpec((tm, d), lambda i: (i, 0)),
                  pl.BlockSpec((1, d), lambda i: (0, 0))],
        out_specs=pl.BlockSpec((tm, d), lambda i: (i, 0)),
        name="final_norm", compiler_params=_cparams(("parallel",)),
    )(x, g.reshape(1, d))


def _attn_body(it_ref, jt_ref, q_ref, k_ref, v_ref, lam_ref, g_ref, o_ref,
               q1_s, q2_s, m1_s, l1_s, a1_s, m2_s, l2_s, a2_s, *, tq, tk, lam_init):
    i = it_ref[pl.program_id(2)]
    j = jt_ref[pl.program_id(2)]

    @pl.when(j == 0)
    def _():
        q = q_ref[...] * (ATT_HD ** -0.5)
        lane = lax.broadcasted_iota(jnp.int32, q.shape, 1)
        q1_s[...] = jnp.where(lane < ATT_HD, q, 0).astype(BF16)
        q2_s[...] = jnp.where(lane >= ATT_HD, q, 0).astype(BF16)
        for m_s, l_s, a_s in ((m1_s, l1_s, a1_s), (m2_s, l2_s, a2_s)):
            m_s[...] = jnp.full(m_s.shape, NEG, F32)
            l_s[...] = jnp.zeros(l_s.shape, F32)
            a_s[...] = jnp.zeros(a_s.shape, F32)

    def step(masked):
        k = k_ref[...]
        v = v_ref[...]
        if masked:
            krow = lax.broadcasted_iota(jnp.int32, (tk, tq), 0)
            qcol = lax.broadcasted_iota(jnp.int32, (tk, tq), 1)
            causal = krow <= qcol
        for q_s, m_s, l_s, a_s in ((q1_s, m1_s, l1_s, a1_s), (q2_s, m2_s, l2_s, a2_s)):
            s = _dot_nt(k, q_s[...])
            if masked:
                s = jnp.where(causal, s, NEG)
            m_old = m_s[...]
            m_new = jnp.maximum(m_old, jnp.max(s, axis=0, keepdims=True))
            alpha = jnp.exp(m_old - m_new)
            p = jnp.exp(s - m_new)
            l_s[...] = alpha * l_s[...] + jnp.sum(p, axis=0, keepdims=True)
            pv = lax.dot_general(v, p.astype(BF16), (((0,), (0,)), ((), ())),
                                 preferred_element_type=F32)
            a_s[...] = alpha * a_s[...] + pv
            m_s[...] = m_new

    @pl.when(j < i)
    def _():
        step(False)

    @pl.when(j == i)
    def _():
        step(True)
        lp = lam_ref[...]
        lam = (jnp.exp(jnp.sum(lp[0:1] * lp[1:2], axis=-1, keepdims=True))
               - jnp.exp(jnp.sum(lp[2:3] * lp[3:4], axis=-1, keepdims=True)) + lam_init)
        o = a1_s[...] / l1_s[...] - lam * (a2_s[...] / l2_s[...])
        o = o * lax.rsqrt(jnp.mean(o * o, axis=0, keepdims=True) + NORM_EPS) * g_ref[...]
        o_ref[...] = (o * (1.0 - lam_init)).T.astype(o_ref.dtype)


def _diff_attention(qkv, lam_p, subln_g, lam_init, bsz, seq, heads):
    t = qkv.shape[0]
    vd = 2 * ATT_HD
    tq = tk = _tile(512, seq)
    nq = seq // tq
    pairs = [(i, j) for i in range(nq) for j in range(i + 1)]
    i_tab = jnp.asarray([p[0] for p in pairs], jnp.int32)
    j_tab = jnp.asarray([p[1] for p in pairs], jnp.int32)
    body = functools.partial(_attn_body, tq=tq, tk=tk, lam_init=lam_init)
    grid_spec = pltpu.PrefetchScalarGridSpec(
        num_scalar_prefetch=2, grid=(bsz, heads, len(pairs)),
        in_specs=[pl.BlockSpec((tq, vd), lambda b, h, p, it, jt: (b * nq + it[p], h)),
                  pl.BlockSpec((tk, vd), lambda b, h, p, it, jt: (b * nq + jt[p], heads + h)),
                  pl.BlockSpec((tk, vd), lambda b, h, p, it, jt: (b * nq + jt[p], 2 * heads + h)),
                  pl.BlockSpec((4, ATT_HD), lambda b, h, p, it, jt: (0, 0)),
                  pl.BlockSpec((vd, 1), lambda b, h, p, it, jt: (0, 0))],
        out_specs=pl.BlockSpec((tq, vd), lambda b, h, p, it, jt: (b * nq + it[p], h)),
        scratch_shapes=[pltpu.VMEM((tq, vd), BF16), pltpu.VMEM((tq, vd), BF16),
                        pltpu.VMEM((1, tq), F32), pltpu.VMEM((1, tq), F32), pltpu.VMEM((vd, tq), F32),
                        pltpu.VMEM((1, tq), F32), pltpu.VMEM((1, tq), F32), pltpu.VMEM((vd, tq), F32)])
    return pl.pallas_call(
        body, out_shape=jax.ShapeDtypeStruct((t, heads * vd), BF16), grid_spec=grid_spec,
        name="diff_attn",
        compiler_params=_cparams(("parallel", "parallel", "arbitrary")),
    )(i_tab, j_tab, qkv, qkv, qkv, lam_p, subln_g.reshape(vd, 1))


def _sgu_body(u_ref, v_ref, lng_ref, lnb_ref, w_ref, bias_ref, o_ref, *, groups, nchunk):
    u = jax.nn.gelu(u_ref[...])
    v = jax.nn.gelu(v_ref[...])
    mu = jnp.mean(v, axis=-1, keepdims=True)
    vc = v - mu
    var = jnp.mean(vc * vc, axis=-1, keepdims=True)
    vn = (vc * lax.rsqrt(var + NORM_EPS) * lng_ref[...] + lnb_ref[...]).astype(BF16)
    ri = lax.broadcasted_iota(jnp.int32, (CHUNK, CHUNK), 0)
    ci = lax.broadcasted_iota(jnp.int32, (CHUNK, CHUNK), 1)
    causal = ci <= ri
    for g in range(groups):
        cs = slice(g * LANES, (g + 1) * LANES)
        wg = jnp.where(causal, w_ref[g], 0.0).astype(BF16)
        bias = bias_ref[:, cs]
        for c in range(nchunk):
            rs = slice(c * CHUNK, (c + 1) * CHUNK)
            mixed = _dot(wg, vn[rs, cs]) + bias
            o_ref[rs, cs] = (u[rs, cs] * mixed).astype(o_ref.dtype)


def _sgu(ug, ln_g, ln_b, w_s, b_s):
    t, w2 = ug.shape
    w = w2 // 2
    groups = w // LANES
    ts = _tile(256, t)
    bias = jnp.repeat(b_s.T, LANES, axis=1)
    body = functools.partial(_sgu_body, groups=groups, nchunk=ts // CHUNK)
    return pl.pallas_call(
        body, out_shape=jax.ShapeDtypeStruct((t, w), BF16), grid=(t // ts,),
        in_specs=[pl.BlockSpec((ts, w), lambda i: (i, 0)),
                  pl.BlockSpec((ts, w), lambda i: (i, 1)),
                  pl.BlockSpec((1, w), lambda i: (0, 0)),
                  pl.BlockSpec((1, w), lambda i: (0, 0)),
                  pl.BlockSpec((groups, CHUNK, CHUNK), lambda i: (0, 0, 0)),
                  pl.BlockSpec((CHUNK, w), lambda i: (0, 0))],
        out_specs=pl.BlockSpec((ts, w), lambda i: (i, 0)),
        name="sgu", compiler_params=_cparams(("parallel",)),
    )(ug, ug, ln_g.reshape(1, w), ln_b.reshape(1, w), w_s, bias)


def _conv_body(x_ref, w_ref, b_ref, o_ref, prev_s):
    @pl.when(pl.program_id(2) == 0)
    def _():
        prev_s[...] = jnp.zeros(prev_s.shape, F32)

    x = x_ref[...]
    prev = prev_s[...]
    row = lax.broadcasted_iota(jnp.int32, x.shape, 0)
    acc = x * w_ref[CONV_K - 1:CONV_K, :] + b_ref[...]
    for k in range(1, CONV_K):
        xs = jnp.where(row < k, pltpu.roll(prev, k, 0), pltpu.roll(x, k, 0))
        acc = acc + xs * w_ref[CONV_K - 1 - k:CONV_K - k, :]
    o_ref[...] = acc * jax.nn.sigmoid(acc)
    prev_s[...] = x


def _conv_silu(xbc, w, b, bsz, seq):
    t, c = xbc.shape
    ts = _tile(256, seq)
    tc = _tile(2048, c)
    ns = seq // ts
    return pl.pallas_call(
        _conv_body, out_shape=jax.ShapeDtypeStruct((t, c), F32), grid=(c // tc, bsz, ns),
        in_specs=[pl.BlockSpec((ts, tc), lambda ci, bi, si: (bi * ns + si, ci)),
                  pl.BlockSpec((CONV_K, tc), lambda ci, bi, si: (0, ci)),
                  pl.BlockSpec((1, tc), lambda ci, bi, si: (0, ci))],
        out_specs=pl.BlockSpec((ts, tc), lambda ci, bi, si: (bi * ns + si, ci)),
        scratch_shapes=[pltpu.VMEM((ts, tc), F32)],
        name="conv_silu", compiler_params=_cparams(("parallel", "parallel", "arbitrary")),
    )(xbc, w, b.reshape(1, c))


def _softplus(x):
    return jnp.maximum(x, 0.0) + jnp.log1p(jnp.exp(-jnp.abs(x)))


def _ssd_body(x_ref, b_ref, c_ref, z_ref, dta_ref, dtb_ref, ba_ref, bb_ref, ala_ref, alb_ref,
              d_ref, ng_ref, o_ref, state_s, y_s, *, hpg):
    n = pl.program_id(2)
    L = CHUNK
    gw = hpg * SSD_HD

    @pl.when(n == 0)
    def _():
        state_s[...] = jnp.zeros(state_s.shape, F32)

    dt = _softplus(dta_ref[0] + ba_ref[0])
    dt_t = _softplus(dtb_ref[0] + bb_ref[0])
    da = dt * (-jnp.exp(ala_ref[0]))
    da_t = dt_t * (-jnp.exp(alb_ref[0]))
    ri = lax.broadcasted_iota(jnp.int32, (L, L), 0)
    ci = lax.broadcasted_iota(jnp.int32, (L, L), 1)
    causal = ci <= ri
    tri = causal.astype(BF16)
    tri_t = (ri <= ci).astype(BF16)
    acs = sum(_dot(tri, p) for p in _split3(da))
    acs_t = sum(_dot(p, tri_t) for p in _split3(da_t))
    acs_last = acs[L - 1:L, :]

    head_of_col = lax.broadcasted_iota(jnp.int32, (hpg, gw), 1) // SSD_HD
    expand = (head_of_col == lax.broadcasted_iota(jnp.int32, (hpg, gw), 0)).astype(BF16)

    def widen(v):
        return sum(_dot(p, expand) for p in _split2(v))

    dt_e = widen(dt)
    dec_e = widen(jnp.exp(acs))
    tail_e = widen(jnp.exp(acs_last - acs))

    x = x_ref[...]
    xdt = x * dt_e
    xdt_b = xdt.astype(BF16)
    bm = b_ref[...]
    cm_b = c_ref[...].astype(BF16)
    bm_b = bm.astype(BF16)
    cb = _dot_nt(cm_b, bm_b)
    state = state_s[...]
    y_s[...] = _dot(cm_b, state.astype(BF16)) * dec_e

    lane = lax.broadcasted_iota(jnp.int32, (L, LANES), 1)
    for pair in range(hpg // 2):
        cs = slice(pair * LANES, (pair + 1) * LANES)
        xp = xdt_b[:, cs]
        acc = None
        for half in range(2):
            r = 2 * pair + half
            seg = acs[:, r:r + 1] - acs_t[r:r + 1, :]
            decay = jnp.exp(jnp.where(causal, seg, NEG))
            mm = (cb * decay).astype(BF16)
            keep = (lane < SSD_HD) if half == 0 else (lane >= SSD_HD)
            part = _dot(mm, jnp.where(keep, xp, 0))
            acc = part if acc is None else acc + part
        y_s[:, cs] += acc

    state_s[...] = state * dec_e[L - 1:L, :] + _dot(bm.T.astype(BF16), (tail_e * xdt).astype(BF16))

    y = y_s[...] + x * d_ref[...]
    z = z_ref[...].astype(F32)
    y = y * (z * jax.nn.sigmoid(z))
    o_ref[...] = _rms(y, ng_ref[...]).astype(o_ref.dtype)


def _ssd_scan(xc, z, dt_raw, dt_bias, a_log, d_skip, norm_g, bsz, seq, d_inner):
    t = xc.shape[0]
    g, n_state = SSD_GROUPS, SSD_STATE
    heads = dt_raw.shape[1]
    hpg = heads // g
    gw = hpg * SSD_HD
    assert gw % LANES == 0 and hpg % 2 == 0
    nc = seq // CHUNK
    dta = dt_raw.reshape(t, g, hpg).transpose(1, 0, 2)
    dtb = dta.transpose(0, 2, 1)
    ba = dt_bias.reshape(g, 1, hpg)
    bb = dt_bias.reshape(g, hpg, 1)
    ala = a_log.reshape(g, 1, hpg)
    alb = a_log.reshape(g, hpg, 1)
    d_e = jnp.repeat(d_skip, SSD_HD).reshape(1, d_inner)
    xcol = gw // LANES
    b0 = d_inner // n_state
    c0 = (d_inner + g * n_state) // n_state
    row = lambda bi, ni: bi * nc + ni
    body = functools.partial(_ssd_body, hpg=hpg)
    return pl.pallas_call(
        body, out_shape=jax.ShapeDtypeStruct((t, d_inner), BF16), grid=(bsz, g, nc),
        in_specs=[pl.BlockSpec((CHUNK, gw), lambda bi, gi, ni: (row(bi, ni), gi)),
                  pl.BlockSpec((CHUNK, n_state), lambda bi, gi, ni: (row(bi, ni), b0 + gi)),
                  pl.BlockSpec((CHUNK, n_state), lambda bi, gi, ni: (row(bi, ni), c0 + gi)),
                  pl.BlockSpec((CHUNK, gw), lambda bi, gi, ni: (row(bi, ni), gi)),
                  pl.BlockSpec((1, CHUNK, hpg), lambda bi, gi, ni: (gi, row(bi, ni), 0)),
                  pl.BlockSpec((1, hpg, CHUNK), lambda bi, gi, ni: (gi, 0, row(bi, ni))),
                  pl.BlockSpec((1, 1, hpg), lambda bi, gi, ni: (gi, 0, 0)),
                  pl.BlockSpec((1, hpg, 1), lambda bi, gi, ni: (gi, 0, 0)),
                  pl.BlockSpec((1, 1, hpg), lambda bi, gi, ni: (gi, 0, 0)),
                  pl.BlockSpec((1, hpg, 1), lambda bi, gi, ni: (gi, 0, 0)),
                  pl.BlockSpec((1, gw), lambda bi, gi, ni: (0, gi)),
                  pl.BlockSpec((1, gw), lambda bi, gi, ni: (0, gi))],
        out_specs=pl.BlockSpec((CHUNK, gw), lambda bi, gi, ni: (row(bi, ni), gi)),
        scratch_shapes=[pltpu.VMEM((n_state, gw), F32), pltpu.VMEM((CHUNK, gw), F32)],
        name="ssd_scan", compiler_params=_cparams(("parallel", "parallel", "arbitrary")),
    )(xc, xc, xc, z, dta, dtb, ba, bb, ala, alb, d_e, norm_g.reshape(1, d_inner))


def _topk_rank(s, k_top):
    n = s.shape[0]
    iota = lax.broadcasted_iota(jnp.int32, s.shape, 0).astype(F32)
    rank = jnp.full(s.shape, float(k_top), F32)
    vals = []
    for k in range(k_top):
        m = jnp.max(s, axis=0, keepdims=True)
        idx = jnp.min(jnp.where(s == m, iota, float(n)), axis=0, keepdims=True)
        hit = iota == idx
        rank = jnp.where(hit, float(k), rank)
        s = jnp.where(hit, -jnp.inf, s)
        vals.append(m)
    return rank, jnp.concatenate(vals, axis=0)


_CANDS = [(a, b) for a in range(PEER_TOPK) for b in range(PEER_TOPK // (a + 1))]
N_CAND = 64


def _cand_tables():
    pa = np.zeros((N_CAND, PEER_TOPK), np.float32)
    pb = np.zeros((N_CAND, PEER_TOPK), np.float32)
    for r, (a, b) in enumerate(_CANDS):
        pa[r, a] = 1.0
        pb[r, b] = 1.0
    return jnp.asarray(pa, BF16), jnp.asarray(pb, BF16), jnp.asarray(pa.T, BF16)


def _pick_rows(onehot, v):
    hi, mid, lo = _split3(v)
    return (_dot(onehot, hi) + _dot(onehot, mid)) + _dot(onehot, lo)


def _route_body(q_ref, k_ref, pa_ref, pb_ref, ga_ref, lk_ref, r2_ref, e1_ref, e2_ref):
    kt = PEER_TOPK
    q = q_ref[...]
    s1 = _dot_nt(k_ref[0, 0], q[:, :N_KEYS])
    s2 = _dot_nt(k_ref[0, 1], q[:, N_KEYS:])
    r1, v1 = _topk_rank(s1, kt)
    r2, v2 = _topk_rank(s2, kt)
    c1 = _pick_rows(pa_ref[...], v1)
    c2 = _pick_rows(pb_ref[...], v2)
    row = lax.broadcasted_iota(jnp.int32, c1.shape, 0)
    cand = jnp.where(row < len(_CANDS), c1 + c2, -jnp.inf)
    crank, _ = _topk_rank(cand, kt)
    sel = crank < float(kt)
    gates = jnp.where(sel, jnp.exp(c1 - v1[0:1]) * jnp.exp(c2 - v2[0:1]), 0.0)
    zsum = jnp.sum(gates, axis=0, keepdims=True)
    count = _dot(ga_ref[...], sel.astype(F32).astype(BF16))
    lk = jnp.zeros(s1.shape, F32)
    for a in range(kt):
        lk = jnp.where(r1 == float(a), count[a:a + 1], lk)
    lk_ref[0] = lk
    r2_ref[0] = r2.astype(r2_ref.dtype)
    e1_ref[0] = jnp.exp(s1 - v1[0:1])
    e2_ref[0] = (jnp.exp(s2 - v2[0:1]) / zsum).astype(e2_ref.dtype)


def _peer_route(q, keys):
    t = q.shape[0]
    tt = _tile(512, t)
    f32_shp = jax.ShapeDtypeStruct((PEER_HEADS, N_KEYS, t), F32)
    bf16_shp = jax.ShapeDtypeStruct((PEER_HEADS, N_KEYS, t), BF16)
    ospec = pl.BlockSpec((1, N_KEYS, tt), lambda i, h: (h, 0, i))
    tab_spec = lambda shape: pl.BlockSpec(shape, lambda i, h: (0, 0))
    pa, pb, ga = _cand_tables()
    return pl.pallas_call(
        _route_body, out_shape=(f32_shp, bf16_shp, f32_shp, bf16_shp), grid=(t // tt, PEER_HEADS),
        in_specs=[pl.BlockSpec((tt, 2 * N_KEYS), lambda i, h: (i, h)),
                  pl.BlockSpec((1, 2, N_KEYS, N_KEYS), lambda i, h: (h, 0, 0, 0)),
                  tab_spec(pa.shape), tab_spec(pb.shape), tab_spec(ga.shape)],
        out_specs=(ospec, ospec, ospec, ospec),
        name="peer_route", compiler_params=_cparams(("parallel", "parallel")),
    )(q, keys, pa, pb, ga)


def _peer_gate_epilogue(acc, o_ref, lk_ref, e1_ref, r2_ref, e2_ref):
    for ii in range(acc.shape[0] // N_KEYS):
        rs = slice(ii * N_KEYS, (ii + 1) * N_KEYS)
        gate = None
        for h in range(PEER_HEADS):
            lk = lk_ref[h, ii:ii + 1, :].astype(BF16)
            e1 = e1_ref[h, ii:ii + 1, :].astype(BF16)
            term = e1 * jnp.where(r2_ref[h] < lk, e2_ref[h], jnp.zeros((), BF16))
            gate = term if gate is None else gate + term
        o_ref[rs, :] = (jax.nn.gelu(acc[rs, :]) * gate.astype(F32)).astype(o_ref.dtype)


def kernel(x, c, ada_w1, ada_w2, ada_b, ada_table, norm_mix, norm_ffn, norm_final, hyb_w_in, hyb_w_out, diff_lam, diff_subln, sgu_ln_g, sgu_ln_b, sgu_w_s, sgu_b_s, ssd_w_in, ssd_conv_w, ssd_conv_b, ssd_dt_bias, ssd_a_log, ssd_d, ssd_norm, ssd_w_out, peer_w_q, peer_keys, peer_u, peer_v):
    bsz, seq, d = x.shape
    depth = ada_table.shape[0]
    t = bsz * seq
    n_experts = peer_u.shape[1]
    d_inner = ssd_norm.shape[1]
    ssd_heads = ssd_a_log.shape[1]
    conv_dim = ssd_conv_b.shape[1]
    sgu_w = sgu_ln_g.shape[1]
    att_w = hyb_w_out.shape[1] - sgu_w
    att_heads = att_w // (2 * ATT_HD)

    mod = _ada(c, ada_w1, ada_w2, ada_b, ada_table)
    xt = x.reshape(t, d)

    def gate_spec(l, col, tn):
        per = d // tn
        return lambda tm: (mod, (1, 1, tn), lambda i, j: (l * bsz + (i * tm) // seq, 0, col * per + j))

    for l in range(depth):
        h = _norm_mod(xt, norm_mix[l], mod, l * bsz, 0, 1, seq)
        if l % 2 == 0:
            e = l // 2
            lam_init = 0.8 - 0.6 * math.exp(-0.3 * l)
            w_in = hyb_w_in[e]
            qkv = _mm(h, w_in[:, :3 * att_w].astype(BF16), tm=1024, tn=1024, tk=d,
                      out_shape=jax.ShapeDtypeStruct((t, 3 * att_w), BF16), name="hyb_qkv")
            ug = _mm(h, w_in[:, 3 * att_w:].astype(BF16), tm=1024, tn=1024, tk=d,
                     out_shape=jax.ShapeDtypeStruct((t, 2 * sgu_w), F32), name="hyb_ug")
            a_out = _diff_attention(qkv, diff_lam[e], diff_subln[e], lam_init, bsz, seq, att_heads)
            s_out = _sgu(ug, sgu_ln_g[e], sgu_ln_b[e], sgu_w_s[e], sgu_b_s[e])
            y_in = jnp.concatenate([a_out, s_out], axis=-1)
            w_out = hyb_w_out[e].astype(BF16)
        else:
            o = l // 2
            w_in = ssd_w_in[o]
            z = _mm(h, w_in[:, :d_inner].astype(BF16), tm=1024, tn=1024, tk=d,
                    out_shape=jax.ShapeDtypeStruct((t, d_inner), F32), name="ssd_z")
            xbc = _mm(h, w_in[:, d_inner:d_inner + conv_dim].astype(BF16), tm=1024, tn=1024, tk=d,
                      out_shape=jax.ShapeDtypeStruct((t, conv_dim), F32), name="ssd_xbc")
            dt_raw = _mm(h, w_in[:, d_inner + conv_dim:].astype(BF16), tm=1024, tn=ssd_heads, tk=d,
                         out_shape=jax.ShapeDtypeStruct((t, ssd_heads), F32), name="ssd_dt")
            xc = _conv_silu(xbc, ssd_conv_w[o], ssd_conv_b[o], bsz, seq)
            y_in = _ssd_scan(xc, z, dt_raw, ssd_dt_bias[o], ssd_a_log[o], ssd_d[o], ssd_norm[o],
                             bsz, seq, d_inner)
            w_out = ssd_w_out[o].astype(BF16)
        tm, tn = _tile(1024, seq), _tile(1024, d)
        xt = _mm(y_in, w_out, tm=tm, tn=tn, tk=2048,
                 out_shape=jax.ShapeDtypeStruct((t, d), F32),
                 extras=[(xt, (tm, tn), lambda i, j: (i, j)), gate_spec(l, 2, tn)(tm)],
                 epilogue=_resid_epilogue, name="mix_out")

        h = _norm_mod(xt, norm_ffn[l], mod, l * bsz, 3, 4, seq)
        q = _mm(h, peer_w_q[l].astype(BF16), tm=1024, tn=1024, tk=d,
                out_shape=jax.ShapeDtypeStruct((t, peer_w_q.shape[2]), BF16), name="peer_q")
        lk, r2, e1, e2 = _peer_route(q, peer_keys[l].astype(BF16))
        te, tt = _tile(1024, n_experts), _tile(512, t)
        ni1 = te // N_KEYS
        head_blk = (PEER_HEADS, ni1, tt)
        full_blk = (PEER_HEADS, N_KEYS, tt)
        wt = _mm(peer_u[l].astype(BF16), h, nt=True, tm=te, tn=tt, tk=d, n_outer=True,
                 out_shape=jax.ShapeDtypeStruct((n_experts, t), BF16),
                 extras=[(lk, head_blk, lambda i, j: (0, i, j)),
                         (e1, head_blk, lambda i, j: (0, i, j)),
                         (r2, full_blk, lambda i, j: (0, 0, j)),
                         (e2, full_blk, lambda i, j: (0, 0, j))],
                 epilogue=_peer_gate_epilogue, name="peer_gate")
        td, tt = _tile(1024, d), _tile(1024, seq)
        xt = _mm(peer_v[l].astype(BF16).T, wt, tm=td, tn=tt, tk=2048,
                 out_shape=jax.ShapeDtypeStruct((t, d), F32),
                 out_block=(tt, td), out_index=lambda i, j: (j, i),
                 extras=[(xt, (tt, td), lambda i, j: (j, i)),
                         (mod, (1, 1, td), lambda i, j, l=l, tt=tt, td=td:
                          (l * bsz + (j * tt) // seq, 0, 5 * (d // td) + i))],
                 epilogue=_resid_t_epilogue, name="peer_out")

    return _final_norm(xt, norm_final).reshape(bsz, seq, d)
```

```python
import functools
import math

import jax
import jax.numpy as jnp
import numpy as np
from jax import lax
from jax.experimental import pallas as pl
from jax.experimental.pallas import tpu as pltpu

F32 = jnp.float32
BF16 = jnp.bfloat16
NORM_EPS = 1e-6
LANES = 128
VMEM_LIMIT = 56 * 1024 * 1024
NEG = -1e30

ATT_HD = 64
SSD_HD = 64
SSD_GROUPS = 8
SSD_STATE = 128
CONV_K = 4
CHUNK = 128
PEER_HEADS = 8
N_KEYS = 128
PEER_TOPK = 16
ADA_N_MOD = 6


def _cparams(sem):
    return pltpu.CompilerParams(dimension_semantics=sem, vmem_limit_bytes=VMEM_LIMIT)


def _tile(pref, dim):
    t = min(pref, dim)
    while dim % t:
        t -= LANES
        assert t > 0, (pref, dim)
    return t


def _split2(a):
    hi = a.astype(BF16)
    lo = (a - hi.astype(F32)).astype(BF16)
    return hi, lo


def _split3(a):
    hi = a.astype(BF16)
    r = a - hi.astype(F32)
    mid = r.astype(BF16)
    lo = (r - mid.astype(F32)).astype(BF16)
    return hi, mid, lo


def _dot(a, b):
    return jnp.dot(a, b, preferred_element_type=F32)


def _dot_nt(a, b):
    return lax.dot_general(a, b, (((1,), (1,)), ((), ())), preferred_element_type=F32)


def _mm_body(*refs, nk, nt, n_extra, epilogue):
    a_ref, b_ref = refs[0], refs[1]
    extra = refs[2:2 + n_extra]
    o_ref = refs[2 + n_extra]
    part = _dot_nt(a_ref[...], b_ref[...]) if nt else _dot(a_ref[...], b_ref[...])
    if nk == 1:
        epilogue(part, o_ref, *extra)
        return
    acc_ref = refs[3 + n_extra]
    k = pl.program_id(2)

    @pl.when(k == 0)
    def _():
        acc_ref[...] = part

    @pl.when(k > 0)
    def _():
        acc_ref[...] += part

    @pl.when(k == nk - 1)
    def _():
        epilogue(acc_ref[...], o_ref, *extra)


def _store_epilogue(acc, o_ref):
    o_ref[...] = acc.astype(o_ref.dtype)


def _mm(a, b, *, nt=False, tm, tn, tk, out_shape, out_block=None, out_index=None,
        extras=(), epilogue=_store_epilogue, n_outer=False, name="mm"):
    m, kdim = a.shape
    n = b.shape[0] if nt else b.shape[1]
    tm, tn, tk = _tile(tm, m), _tile(tn, n), _tile(tk, kdim)
    nm, nn, nk = m // tm, n // tn, kdim // tk
    if n_outer:
        grid = (nn, nm, nk)
        ij = lambda g0, g1: (g1, g0)
    else:
        grid = (nm, nn, nk)
        ij = lambda g0, g1: (g0, g1)

    def wrap(fn):
        return lambda g0, g1, k: fn(*ij(g0, g1), k)

    a_spec = pl.BlockSpec((tm, tk), wrap(lambda i, j, k: (i, k)))
    if nt:
        b_spec = pl.BlockSpec((tn, tk), wrap(lambda i, j, k: (j, k)))
    else:
        b_spec = pl.BlockSpec((tk, tn), wrap(lambda i, j, k: (k, j)))
    extra_specs = [pl.BlockSpec(blk, wrap(lambda i, j, k, f=f: f(i, j))) for _, blk, f in extras]
    if out_block is None:
        out_block, out_index = (tm, tn), (lambda i, j: (i, j))
    o_spec = pl.BlockSpec(out_block, wrap(lambda i, j, k: out_index(i, j)))
    scratch = [pltpu.VMEM((tm, tn), F32)] if nk > 1 else []
    body = functools.partial(_mm_body, nk=nk, nt=nt, n_extra=len(extras), epilogue=epilogue)
    return pl.pallas_call(
        body, out_shape=out_shape, grid=grid,
        in_specs=[a_spec, b_spec] + extra_specs, out_specs=o_spec,
        scratch_shapes=scratch, name=name,
        compiler_params=_cparams(("parallel", "parallel", "arbitrary")),
    )(a, b, *[e[0] for e in extras])


def _resid_epilogue(acc, o_ref, x_ref, g_ref):
    o_ref[...] = x_ref[...] + g_ref[0] * acc


def _resid_t_epilogue(acc, o_ref, x_ref, g_ref):
    o_ref[...] = x_ref[...] + g_ref[0] * acc.T


def _dot_split(a, b):
    a_hi, a_lo = _split2(a)
    b_hi, b_lo = _split2(b)
    return _dot(a_hi, b_hi) + _dot(a_hi, b_lo) + _dot(a_lo, b_hi)


def _ada_body(c_ref, w1_ref, w2_ref, b_ref, tab_ref, o_ref):
    c = c_ref[...]
    t = c * jax.nn.sigmoid(c)
    t1 = _dot_split(t, w1_ref[...])
    t0 = _dot_split(t1, w2_ref[...]) + b_ref[...]
    for l in range(o_ref.shape[0]):
        o_ref[l] = t0 + tab_ref[l:l + 1, :]


def _ada(c, w1, w2, b, table):
    bsz, d = c.shape
    depth, n = table.shape
    rank = w1.shape[1]
    bp = 8 * ((bsz + 7) // 8)
    cp = jnp.zeros((bp, d), F32).at[:bsz].set(c)
    tn = _tile(2048, d)
    out = pl.pallas_call(
        _ada_body, out_shape=jax.ShapeDtypeStruct((depth, bp, n), F32), grid=(n // tn,),
        in_specs=[pl.BlockSpec((bp, d), lambda j: (0, 0)),
                  pl.BlockSpec((d, rank), lambda j: (0, 0)),
                  pl.BlockSpec((rank, tn), lambda j: (0, j)),
                  pl.BlockSpec((1, tn), lambda j: (0, j)),
                  pl.BlockSpec((depth, tn), lambda j: (0, j))],
        out_specs=pl.BlockSpec((depth, bp, tn), lambda j: (0, 0, j)),
        name="ada", compiler_params=_cparams(("parallel",)),
    )(cp, w1, w2, b.reshape(1, n), table)
    return out[:, :bsz].reshape(depth * bsz, 1, n)


def _rms(x, g):
    return x * lax.rsqrt(jnp.mean(x * x, axis=-1, keepdims=True) + NORM_EPS) * g


def _norm_mod_body(x_ref, g_ref, sc_ref, sh_ref, o_ref):
    y = _rms(x_ref[...], g_ref[...])
    o_ref[...] = (y * (1.0 + sc_ref[0]) + sh_ref[0]).astype(o_ref.dtype)


def _norm_body(x_ref, g_ref, o_ref):
    o_ref[...] = _rms(x_ref[...], g_ref[...]).astype(o_ref.dtype)


def _norm_mod(x, g, mod, row0, sh_col, sc_col, seq):
    t, d = x.shape
    tm = _tile(256, seq)
    return pl.pallas_call(
        _norm_mod_body, out_shape=jax.ShapeDtypeStruct((t, d), BF16), grid=(t // tm,),
        in_specs=[pl.BlockSpec((tm, d), lambda i: (i, 0)),
                  pl.BlockSpec((1, d), lambda i: (0, 0)),
                  pl.BlockSpec((1, 1, d), lambda i: (row0 + (i * tm) // seq, 0, sc_col)),
                  pl.BlockSpec((1, 1, d), lambda i: (row0 + (i * tm) // seq, 0, sh_col))],
        out_specs=pl.BlockSpec((tm, d), lambda i: (i, 0)),
        name="norm_mod", compiler_params=_cparams(("parallel",)),
    )(x, g.reshape(1, d), mod, mod)


def _final_norm(x, g):
    t, d = x.shape
    tm = _tile(256, t)
    return pl.pallas_call(
        _norm_body, out_shape=jax.ShapeDtypeStruct((t, d), F32), grid=(t // tm,),
        in_specs=[pl.BlockSpec((tm, d), lambda i: (i, 0)),
                  pl.BlockSpec((1, d), lambda i: (0, 0))],
        out_specs=pl.BlockSpec((tm, d), lambda i: (i, 0)),
        name="final_norm", compiler_params=_cparams(("parallel",)),
    )(x, g.reshape(1, d))


def _attn_body(it_ref, jt_ref, q_ref, k_ref, v_ref, lam_ref, g_ref, o_ref,
               q_s, m_s, l_s, a_s, *, tq, tk, hb, lam_init):
    i = it_ref[pl.program_id(2)]
    j = jt_ref[pl.program_id(2)]
    vd = 2 * ATT_HD

    @pl.when(j == 0)
    def _():
        q = q_ref[...] * (ATT_HD ** -0.5)
        lane = lax.broadcasted_iota(jnp.int32, (tq, vd), 1)
        for hh in range(hb):
            qh = q[:, hh * vd:(hh + 1) * vd]
            q_s[2 * hh] = jnp.where(lane < ATT_HD, qh, 0).astype(BF16)
            q_s[2 * hh + 1] = jnp.where(lane >= ATT_HD, qh, 0).astype(BF16)
        m_s[...] = jnp.full(m_s.shape, NEG, F32)
        l_s[...] = jnp.zeros(l_s.shape, F32)
        a_s[...] = jnp.zeros(a_s.shape, F32)

    def step(masked):
        if masked:
            krow = lax.broadcasted_iota(jnp.int32, (tk, tq), 0)
            qcol = lax.broadcasted_iota(jnp.int32, (tk, tq), 1)
            causal = krow <= qcol
        for hh in range(hb):
            k = k_ref[:, hh * vd:(hh + 1) * vd]
            v = v_ref[:, hh * vd:(hh + 1) * vd]
            for st in (2 * hh, 2 * hh + 1):
                s = _dot_nt(k, q_s[st])
                if masked:
                    s = jnp.where(causal, s, NEG)
                m_old = m_s[st]
                m_new = jnp.maximum(m_old, jnp.max(s, axis=0, keepdims=True))
                alpha = jnp.exp(m_old - m_new)
                p = jnp.exp(s - m_new)
                l_s[st] = alpha * l_s[st] + jnp.sum(p, axis=0, keepdims=True)
                pv = lax.dot_general(v, p.astype(BF16), (((0,), (0,)), ((), ())),
                                     preferred_element_type=F32)
                a_s[st] = alpha * a_s[st] + pv
                m_s[st] = m_new

    @pl.when(j < i)
    def _():
        step(False)

    @pl.when(j == i)
    def _():
        step(True)
        lp = lam_ref[...]
        lam = (jnp.exp(jnp.sum(lp[0:1] * lp[1:2], axis=-1, keepdims=True))
               - jnp.exp(jnp.sum(lp[2:3] * lp[3:4], axis=-1, keepdims=True)) + lam_init)
        for hh in range(hb):
            o = a_s[2 * hh] / l_s[2 * hh] - lam * (a_s[2 * hh + 1] / l_s[2 * hh + 1])
            o = o * lax.rsqrt(jnp.mean(o * o, axis=0, keepdims=True) + NORM_EPS) * g_ref[...]
            o_ref[:, hh * vd:(hh + 1) * vd] = (o * (1.0 - lam_init)).T.astype(o_ref.dtype)


def _diff_attention(qkv, lam_p, subln_g, lam_init, bsz, seq, heads):
    t = qkv.shape[0]
    vd = 2 * ATT_HD
    hb = 2 if heads % 2 == 0 else 1
    hblk = heads // hb
    tq = tk = _tile(512, seq)
    nq = seq // tq
    pairs = [(i, j) for i in range(nq) for j in range(i + 1)]
    i_tab = jnp.asarray([p[0] for p in pairs], jnp.int32)
    j_tab = jnp.asarray([p[1] for p in pairs], jnp.int32)
    body = functools.partial(_attn_body, tq=tq, tk=tk, hb=hb, lam_init=lam_init)
    grid_spec = pltpu.PrefetchScalarGridSpec(
        num_scalar_prefetch=2, grid=(bsz, hblk, len(pairs)),
        in_specs=[pl.BlockSpec((tq, hb * vd), lambda b, h, p, it, jt: (b * nq + it[p], h)),
                  pl.BlockSpec((tk, hb * vd), lambda b, h, p, it, jt: (b * nq + jt[p], hblk + h)),
                  pl.BlockSpec((tk, hb * vd), lambda b, h, p, it, jt: (b * nq + jt[p], 2 * hblk + h)),
                  pl.BlockSpec((4, ATT_HD), lambda b, h, p, it, jt: (0, 0)),
                  pl.BlockSpec((vd, 1), lambda b, h, p, it, jt: (0, 0))],
        out_specs=pl.BlockSpec((tq, hb * vd), lambda b, h, p, it, jt: (b * nq + it[p], h)),
        scratch_shapes=[pltpu.VMEM((2 * hb, tq, vd), BF16), pltpu.VMEM((2 * hb, 1, tq), F32),
                        pltpu.VMEM((2 * hb, 1, tq), F32), pltpu.VMEM((2 * hb, vd, tq), F32)])
    return pl.pallas_call(
        body, out_shape=jax.ShapeDtypeStruct((t, heads * vd), BF16), grid_spec=grid_spec,
        name="diff_attn",
        compiler_params=_cparams(("parallel", "parallel", "arbitrary")),
    )(i_tab, j_tab, qkv, qkv, qkv, lam_p, subln_g.reshape(vd, 1))


def _sgu_body(u_ref, v_ref, lng_ref, lnb_ref, w_ref, bias_ref, o_ref, *, groups, nchunk):
    u = jax.nn.gelu(u_ref[...])
    v = jax.nn.gelu(v_ref[...])
    mu = jnp.mean(v, axis=-1, keepdims=True)
    vc = v - mu
    var = jnp.mean(vc * vc, axis=-1, keepdims=True)
    vn = (vc * lax.rsqrt(var + NORM_EPS) * lng_ref[...] + lnb_ref[...]).astype(BF16)
    ri = lax.broadcasted_iota(jnp.int32, (CHUNK, CHUNK), 0)
    ci = lax.broadcasted_iota(jnp.int32, (CHUNK, CHUNK), 1)
    causal = ci <= ri
    for g in range(groups):
        cs = slice(g * LANES, (g + 1) * LANES)
        wg = jnp.where(causal, w_ref[g], 0.0).astype(BF16)
        bias = bias_ref[:, cs]
        for c in range(nchunk):
            rs = slice(c * CHUNK, (c + 1) * CHUNK)
            mixed = _dot(wg, vn[rs, cs]) + bias
            o_ref[rs, cs] = (u[rs, cs] * mixed).astype(o_ref.dtype)


def _sgu(ug, ln_g, ln_b, w_s, b_s):
    t, w2 = ug.shape
    w = w2 // 2
    groups = w // LANES
    ts = _tile(256, t)
    bias = jnp.repeat(b_s.T, LANES, axis=1)
    body = functools.partial(_sgu_body, groups=groups, nchunk=ts // CHUNK)
    return pl.pallas_call(
        body, out_shape=jax.ShapeDtypeStruct((t, w), BF16), grid=(t // ts,),
        in_specs=[pl.BlockSpec((ts, w), lambda i: (i, 0)),
                  pl.BlockSpec((ts, w), lambda i: (i, 1)),
                  pl.BlockSpec((1, w), lambda i: (0, 0)),
                  pl.BlockSpec((1, w), lambda i: (0, 0)),
                  pl.BlockSpec((groups, CHUNK, CHUNK), lambda i: (0, 0, 0)),
                  pl.BlockSpec((CHUNK, w), lambda i: (0, 0))],
        out_specs=pl.BlockSpec((ts, w), lambda i: (i, 0)),
        name="sgu", compiler_params=_cparams(("parallel",)),
    )(ug, ug, ln_g.reshape(1, w), ln_b.reshape(1, w), w_s, bias)


def _conv_body(x_ref, w_ref, b_ref, o_ref, prev_s):
    @pl.when(pl.program_id(2) == 0)
    def _():
        prev_s[...] = jnp.zeros(prev_s.shape, F32)

    x = x_ref[...]
    prev = prev_s[...]
    row = lax.broadcasted_iota(jnp.int32, x.shape, 0)
    acc = x * w_ref[CONV_K - 1:CONV_K, :] + b_ref[...]
    for k in range(1, CONV_K):
        xs = jnp.where(row < k, pltpu.roll(prev, k, 0), pltpu.roll(x, k, 0))
        acc = acc + xs * w_ref[CONV_K - 1 - k:CONV_K - k, :]
    o_ref[...] = acc * jax.nn.sigmoid(acc)
    prev_s[...] = x


def _conv_silu(xbc, w, b, bsz, seq):
    t, c = xbc.shape
    ts = _tile(256, seq)
    tc = _tile(2048, c)
    ns = seq // ts
    return pl.pallas_call(
        _conv_body, out_shape=jax.ShapeDtypeStruct((t, c), F32), grid=(c // tc, bsz, ns),
        in_specs=[pl.BlockSpec((ts, tc), lambda ci, bi, si: (bi * ns + si, ci)),
                  pl.BlockSpec((CONV_K, tc), lambda ci, bi, si: (0, ci)),
                  pl.BlockSpec((1, tc), lambda ci, bi, si: (0, ci))],
        out_specs=pl.BlockSpec((ts, tc), lambda ci, bi, si: (bi * ns + si, ci)),
        scratch_shapes=[pltpu.VMEM((ts, tc), F32)],
        name="conv_silu", compiler_params=_cparams(("parallel", "parallel", "arbitrary")),
    )(xbc, w, b.reshape(1, c))


def _softplus(x):
    return jnp.maximum(x, 0.0) + jnp.log1p(jnp.exp(-jnp.abs(x)))


def _ssd_body(x_ref, b_ref, c_ref, z_ref, dta_ref, dtb_ref, ba_ref, bb_ref, ala_ref, alb_ref,
              d_ref, ng_ref, o_ref, state_s, y_s, *, hpg):
    n = pl.program_id(2)
    L = CHUNK
    gw = hpg * SSD_HD

    @pl.when(n == 0)
    def _():
        state_s[...] = jnp.zeros(state_s.shape, F32)

    dt = _softplus(dta_ref[0] + ba_ref[0])
    dt_t = _softplus(dtb_ref[0] + bb_ref[0])
    da = dt * (-jnp.exp(ala_ref[0]))
    da_t = dt_t * (-jnp.exp(alb_ref[0]))
    ri = lax.broadcasted_iota(jnp.int32, (L, L), 0)
    ci = lax.broadcasted_iota(jnp.int32, (L, L), 1)
    causal = ci <= ri
    tri = causal.astype(BF16)
    tri_t = (ri <= ci).astype(BF16)
    acs = sum(_dot(tri, p) for p in _split3(da))
    acs_t = sum(_dot(p, tri_t) for p in _split3(da_t))
    acs_last = acs[L - 1:L, :]

    head_of_col = lax.broadcasted_iota(jnp.int32, (hpg, gw), 1) // SSD_HD
    expand = (head_of_col == lax.broadcasted_iota(jnp.int32, (hpg, gw), 0)).astype(BF16)

    def widen(v):
        return sum(_dot(p, expand) for p in _split2(v))

    dt_e = widen(dt)
    dec_e = widen(jnp.exp(acs))
    tail_e = widen(jnp.exp(acs_last - acs))

    x = x_ref[...]
    xdt = x * dt_e
    xdt_b = xdt.astype(BF16)
    bm = b_ref[...]
    cm_b = c_ref[...].astype(BF16)
    bm_b = bm.astype(BF16)
    cb = _dot_nt(cm_b, bm_b)
    state = state_s[...]
    y_s[...] = _dot(cm_b, state.astype(BF16)) * dec_e

    lane = lax.broadcasted_iota(jnp.int32, (L, LANES), 1)
    for pair in range(hpg // 2):
        cs = slice(pair * LANES, (pair + 1) * LANES)
        xp = xdt_b[:, cs]
        acc = None
        for half in range(2):
            r = 2 * pair + half
            seg = acs[:, r:r + 1] - acs_t[r:r + 1, :]
            decay = jnp.exp(jnp.where(causal, seg, NEG))
            mm = (cb * decay).astype(BF16)
            keep = (lane < SSD_HD) if half == 0 else (lane >= SSD_HD)
            part = _dot(mm, jnp.where(keep, xp, 0))
            acc = part if acc is None else acc + part
        y_s[:, cs] += acc

    state_s[...] = state * dec_e[L - 1:L, :] + _dot(bm.T.astype(BF16), (tail_e * xdt).astype(BF16))

    y = y_s[...] + x * d_ref[...]
    z = z_ref[...].astype(F32)
    y = y * (z * jax.nn.sigmoid(z))
    o_ref[...] = _rms(y, ng_ref[...]).astype(o_ref.dtype)


def _ssd_scan(xc, z, dt_raw, dt_bias, a_log, d_skip, norm_g, bsz, seq, d_inner):
    t = xc.shape[0]
    g, n_state = SSD_GROUPS, SSD_STATE
    heads = dt_raw.shape[1]
    hpg = heads // g
    gw = hpg * SSD_HD
    assert gw % LANES == 0 and hpg % 2 == 0
    nc = seq // CHUNK
    dta = dt_raw.reshape(t, g, hpg).transpose(1, 0, 2)
    dtb = dta.transpose(0, 2, 1)
    ba = dt_bias.reshape(g, 1, hpg)
    bb = dt_bias.reshape(g, hpg, 1)
    ala = a_log.reshape(g, 1, hpg)
    alb = a_log.reshape(g, hpg, 1)
    d_e = jnp.repeat(d_skip, SSD_HD).reshape(1, d_inner)
    xcol = gw // LANES
    b0 = d_inner // n_state
    c0 = (d_inner + g * n_state) // n_state
    row = lambda bi, ni: bi * nc + ni
    body = functools.partial(_ssd_body, hpg=hpg)
    return pl.pallas_call(
        body, out_shape=jax.ShapeDtypeStruct((t, d_inner), BF16), grid=(bsz, g, nc),
        in_specs=[pl.BlockSpec((CHUNK, gw), lambda bi, gi, ni: (row(bi, ni), gi)),
                  pl.BlockSpec((CHUNK, n_state), lambda bi, gi, ni: (row(bi, ni), b0 + gi)),
                  pl.BlockSpec((CHUNK, n_state), lambda bi, gi, ni: (row(bi, ni), c0 + gi)),
                  pl.BlockSpec((CHUNK, gw), lambda bi, gi, ni: (row(bi, ni), gi)),
                  pl.BlockSpec((1, CHUNK, hpg), lambda bi, gi, ni: (gi, row(bi, ni), 0)),
                  pl.BlockSpec((1, hpg, CHUNK), lambda bi, gi, ni: (gi, 0, row(bi, ni))),
                  pl.BlockSpec((1, 1, hpg), lambda bi, gi, ni: (gi, 0, 0)),
                  pl.BlockSpec((1, hpg, 1), lambda bi, gi, ni: (gi, 0, 0)),
                  pl.BlockSpec((1, 1, hpg), lambda bi, gi, ni: (gi, 0, 0)),
                  pl.BlockSpec((1, hpg, 1), lambda bi, gi, ni: (gi, 0, 0)),
                  pl.BlockSpec((1, gw), lambda bi, gi, ni: (0, gi)),
                  pl.BlockSpec((1, gw), lambda bi, gi, ni: (0, gi))],
        out_specs=pl.BlockSpec((CHUNK, gw), lambda bi, gi, ni: (row(bi, ni), gi)),
        scratch_shapes=[pltpu.VMEM((n_state, gw), F32), pltpu.VMEM((CHUNK, gw), F32)],
        name="ssd_scan", compiler_params=_cparams(("parallel", "parallel", "arbitrary")),
    )(xc, xc, xc, z, dta, dtb, ba, bb, ala, alb, d_e, norm_g.reshape(1, d_inner))


def _topk_rank(s, k_top, exact):
    n = s.shape[0]
    iota = lax.broadcasted_iota(jnp.int32, s.shape, 0).astype(F32)
    rank = jnp.full(s.shape, float(k_top), F32)
    vals = []
    for k in range(k_top):
        m = jnp.max(s, axis=0, keepdims=True)
        hit = s == m
        if exact:
            idx = jnp.min(jnp.where(hit, iota, float(n)), axis=0, keepdims=True)
            hit = iota == idx
        rank = jnp.where(hit, float(k), rank)
        s = jnp.where(hit, -jnp.inf, s)
        vals.append(m)
    return rank, jnp.concatenate(vals, axis=0)


_CANDS = [(a, b) for a in range(PEER_TOPK) for b in range(PEER_TOPK // (a + 1))]
N_CAND = 64


def _cand_tables():
    pa = np.zeros((N_CAND, PEER_TOPK), np.float32)
    pb = np.zeros((N_CAND, PEER_TOPK), np.float32)
    for r, (a, b) in enumerate(_CANDS):
        pa[r, a] = 1.0
        pb[r, b] = 1.0
    return jnp.asarray(pa, BF16), jnp.asarray(pb, BF16), jnp.asarray(pa.T, BF16)


def _pick_rows(onehot, v):
    hi, mid, lo = _split3(v)
    return (_dot(onehot, hi) + _dot(onehot, mid)) + _dot(onehot, lo)


def _route_compute(s1, s2, pa, pb, ga, exact):
    kt = PEER_TOPK
    r1, v1 = _topk_rank(s1, kt, exact)
    r2, v2 = _topk_rank(s2, kt, exact)
    c1 = _pick_rows(pa, v1)
    c2 = _pick_rows(pb, v2)
    row = lax.broadcasted_iota(jnp.int32, c1.shape, 0)
    cand = jnp.where(row < len(_CANDS), c1 + c2, -jnp.inf)
    crank, _ = _topk_rank(cand, kt, exact)
    sel = crank < float(kt)
    gates = jnp.where(sel, jnp.exp(c1 - v1[0:1]) * jnp.exp(c2 - v2[0:1]), 0.0)
    zsum = jnp.sum(gates, axis=0, keepdims=True)
    count = _dot(ga, sel.astype(F32).astype(BF16))
    lk = jnp.zeros(s1.shape, F32)
    for a in range(kt):
        lk = jnp.where(r1 == float(a), count[a:a + 1], lk)
    e1 = jnp.exp(s1 - v1[0:1])
    e2 = jnp.exp(s2 - v2[0:1]) / zsum * 0.5
    nsel = sum(jnp.sum((r < float(kt)).astype(F32), axis=0, keepdims=True) for r in (r1, r2, crank))
    return (lk, r2, e1, e2), nsel


def _route_body(q_ref, k_ref, pa_ref, pb_ref, ga_ref, lk_ref, r2_ref, e1_ref, e2_ref):
    q = q_ref[...]
    s1 = _dot_nt(k_ref[0, 0], q[:, :N_KEYS])
    s2 = _dot_nt(k_ref[0, 1], q[:, N_KEYS:])
    tabs = (pa_ref[...], pb_ref[...], ga_ref[...])

    def store(outs):
        for ref, val in zip((lk_ref, r2_ref, e1_ref, e2_ref), outs):
            ref[0] = val.astype(ref.dtype)

    outs, nsel = _route_compute(s1, s2, *tabs, exact=False)
    tied = jnp.max(jnp.abs(nsel - 3.0 * PEER_TOPK)) > 0.0

    @pl.when(jnp.logical_not(tied))
    def _():
        store(outs)

    @pl.when(tied)
    def _():
        store(_route_compute(s1, s2, *tabs, exact=True)[0])


def _peer_route(q, keys):
    t = q.shape[0]
    tt = _tile(512, t)
    f32_shp = jax.ShapeDtypeStruct((PEER_HEADS, N_KEYS, t), F32)
    bf16_shp = jax.ShapeDtypeStruct((PEER_HEADS, N_KEYS, t), BF16)
    ospec = pl.BlockSpec((1, N_KEYS, tt), lambda i, h: (h, 0, i))
    tab_spec = lambda shape: pl.BlockSpec(shape, lambda i, h: (0, 0))
    pa, pb, ga = _cand_tables()
    return pl.pallas_call(
        _route_body, out_shape=(f32_shp, bf16_shp, f32_shp, bf16_shp), grid=(t // tt, PEER_HEADS),
        in_specs=[pl.BlockSpec((tt, 2 * N_KEYS), lambda i, h: (i, h)),
                  pl.BlockSpec((1, 2, N_KEYS, N_KEYS), lambda i, h: (h, 0, 0, 0)),
                  tab_spec(pa.shape), tab_spec(pb.shape), tab_spec(ga.shape)],
        out_specs=(ospec, ospec, ospec, ospec),
        name="peer_route", compiler_params=_cparams(("parallel", "parallel")),
    )(q, keys, pa, pb, ga)


def _gelu_tanh(x, half_scale):
    c0 = math.sqrt(2.0 / math.pi)
    inner = x * (c0 + (c0 * 0.044715) * (x * x))
    return (x * half_scale) * (1.0 + jnp.tanh(inner))


def _peer_gate_epilogue(acc, o_ref, lk_ref, e1_ref, r2_ref, e2_ref):
    for ii in range(acc.shape[0] // N_KEYS):
        gate = None
        for h in range(PEER_HEADS):
            lk = lk_ref[h, ii:ii + 1, :].astype(BF16)
            e1 = e1_ref[h, ii:ii + 1, :].astype(BF16)
            term = e1 * jnp.where(r2_ref[h] < lk, e2_ref[h], jnp.zeros((), BF16))
            gate = term if gate is None else gate + term
        w = _gelu_tanh(acc[ii * N_KEYS:(ii + 1) * N_KEYS, :], gate.astype(F32))
        o_ref[ii * N_KEYS:(ii + 1) * N_KEYS, :] = w.astype(o_ref.dtype)


def kernel(x, c, ada_w1, ada_w2, ada_b, ada_table, norm_mix, norm_ffn, norm_final, hyb_w_in, hyb_w_out, diff_lam, diff_subln, sgu_ln_g, sgu_ln_b, sgu_w_s, sgu_b_s, ssd_w_in, ssd_conv_w, ssd_conv_b, ssd_dt_bias, ssd_a_log, ssd_d, ssd_norm, ssd_w_out, peer_w_q, peer_keys, peer_u, peer_v):
    bsz, seq, d = x.shape
    depth = ada_table.shape[0]
    t = bsz * seq
    n_experts = peer_u.shape[1]
    d_inner = ssd_norm.shape[1]
    ssd_heads = ssd_a_log.shape[1]
    conv_dim = ssd_conv_b.shape[1]
    sgu_w = sgu_ln_g.shape[1]
    att_w = hyb_w_out.shape[1] - sgu_w
    att_heads = att_w // (2 * ATT_HD)

    mod = _ada(c, ada_w1, ada_w2, ada_b, ada_table)
    xt = x.reshape(t, d)

    def gate_spec(l, col, tn):
        per = d // tn
        return lambda tm: (mod, (1, 1, tn), lambda i, j: (l * bsz + (i * tm) // seq, 0, col * per + j))

    for l in range(depth):
        h = _norm_mod(xt, norm_mix[l], mod, l * bsz, 0, 1, seq)
        if l % 2 == 0:
            e = l // 2
            lam_init = 0.8 - 0.6 * math.exp(-0.3 * l)
            w_in = hyb_w_in[e]
            qkv = _mm(h, w_in[:, :3 * att_w].astype(BF16), tm=1024, tn=1024, tk=d,
                      out_shape=jax.ShapeDtypeStruct((t, 3 * att_w), BF16), name="hyb_qkv")
            ug = _mm(h, w_in[:, 3 * att_w:].astype(BF16), tm=1024, tn=1024, tk=d,
                     out_shape=jax.ShapeDtypeStruct((t, 2 * sgu_w), F32), name="hyb_ug")
            a_out = _diff_attention(qkv, diff_lam[e], diff_subln[e], lam_init, bsz, seq, att_heads)
            s_out = _sgu(ug, sgu_ln_g[e], sgu_ln_b[e], sgu_w_s[e], sgu_b_s[e])
            y_in = jnp.concatenate([a_out, s_out], axis=-1)
            w_out = hyb_w_out[e].astype(BF16)
        else:
            o = l // 2
            w_in = ssd_w_in[o]
            z = _mm(h, w_in[:, :d_inner].astype(BF16), tm=1024, tn=1024, tk=d,
                    out_shape=jax.ShapeDtypeStruct((t, d_inner), F32), name="ssd_z")
            xbc = _mm(h, w_in[:, d_inner:d_inner + conv_dim].astype(BF16), tm=1024, tn=1024, tk=d,
                      out_shape=jax.ShapeDtypeStruct((t, conv_dim), F32), name="ssd_xbc")
            dt_raw = _mm(h, w_in[:, d_inner + conv_dim:].astype(BF16), tm=1024, tn=ssd_heads, tk=d,
                         out_shape=jax.ShapeDtypeStruct((t, ssd_heads), F32), name="ssd_dt")
            xc = _conv_silu(xbc, ssd_conv_w[o], ssd_conv_b[o], bsz, seq)
            y_in = _ssd_scan(xc, z, dt_raw, ssd_dt_bias[o], ssd_a_log[o], ssd_d[o], ssd_norm[o],
                             bsz, seq, d_inner)
            w_out = ssd_w_out[o].astype(BF16)
        tm, tn = _tile(1024, seq), _tile(1024, d)
        xt = _mm(y_in, w_out, tm=tm, tn=tn, tk=2048,
                 out_shape=jax.ShapeDtypeStruct((t, d), F32),
                 extras=[(xt, (tm, tn), lambda i, j: (i, j)), gate_spec(l, 2, tn)(tm)],
                 epilogue=_resid_epilogue, name="mix_out")

        h = _norm_mod(xt, norm_ffn[l], mod, l * bsz, 3, 4, seq)
        q = _mm(h, peer_w_q[l].astype(BF16), tm=1024, tn=1024, tk=d,
                out_shape=jax.ShapeDtypeStruct((t, peer_w_q.shape[2]), BF16), name="peer_q")
        lk, r2, e1, e2 = _peer_route(q, peer_keys[l].astype(BF16))
        te, tt = _tile(1024, n_experts), _tile(512, t)
        ni1 = te // N_KEYS
        head_blk = (PEER_HEADS, ni1, tt)
        full_blk = (PEER_HEADS, N_KEYS, tt)
        wt = _mm(peer_u[l].astype(BF16), h, nt=True, tm=te, tn=tt, tk=d, n_outer=True,
                 out_shape=jax.ShapeDtypeStruct((n_experts, t), BF16),
                 extras=[(lk, head_blk, lambda i, j: (0, i, j)),
                         (e1, head_blk, lambda i, j: (0, i, j)),
                         (r2, full_blk, lambda i, j: (0, 0, j)),
                         (e2, full_blk, lambda i, j: (0, 0, j))],
                 epilogue=_peer_gate_epilogue, name="peer_gate")
        td, tt = _tile(1024, d), _tile(1024, seq)
        xt = _mm(peer_v[l].astype(BF16).T, wt, tm=td, tn=tt, tk=2048,
                 out_shape=jax.ShapeDtypeStruct((t, d), F32),
                 out_block=(tt, td), out_index=lambda i, j: (j, i),
                 extras=[(xt, (tt, td), lambda i, j: (j, i)),
                         (mod, (1, 1, td), lambda i, j, l=l, tt=tt, td=td:
                          (l * bsz + (j * tt) // seq, 0, 5 * (d // td) + i))],
                 epilogue=_resid_t_epilogue, name="peer_out")

    return _final_norm(xt, norm_final).reshape(bsz, seq, d)
```

```python
import functools
import math

import jax
import jax.numpy as jnp
import numpy as np
from jax import lax
from jax.experimental import pallas as pl
from jax.experimental.pallas import tpu as pltpu

F32 = jnp.float32
BF16 = jnp.bfloat16
NORM_EPS = 1e-6
LANES = 128
VMEM_LIMIT = 56 * 1024 * 1024
NEG = -1e30

ATT_HD = 64
SSD_HD = 64
SSD_GROUPS = 8
SSD_STATE = 128
CONV_K = 4
CHUNK = 128
PEER_HEADS = 8
N_KEYS = 128
PEER_TOPK = 16
ADA_N_MOD = 6


def _cparams(sem):
    return pltpu.CompilerParams(dimension_semantics=sem, vmem_limit_bytes=VMEM_LIMIT)


def _tile(pref, dim):
    t = min(pref, dim)
    while dim % t:
        t -= LANES
        assert t > 0, (pref, dim)
    return t


def _split2(a):
    hi = a.astype(BF16)
    lo = (a - hi.astype(F32)).astype(BF16)
    return hi, lo


def _split3(a):
    hi = a.astype(BF16)
    r = a - hi.astype(F32)
    mid = r.astype(BF16)
    lo = (r - mid.astype(F32)).astype(BF16)
    return hi, mid, lo


def _dot(a, b):
    return jnp.dot(a, b, preferred_element_type=F32)


def _dot_nt(a, b):
    return lax.dot_general(a, b, (((1,), (1,)), ((), ())), preferred_element_type=F32)


def _mm_body(*refs, nk, nt, n_extra, epilogue):
    a_ref, b_ref = refs[0], refs[1]
    extra = refs[2:2 + n_extra]
    o_ref = refs[2 + n_extra]
    dot = _dot_nt if nt else _dot
    if nk == 1:
        epilogue(dot(a_ref[...], b_ref[...]), o_ref, *extra)
        return
    acc_ref = refs[3 + n_extra]
    k = pl.program_id(2)

    @pl.when(k == 0)
    def _():
        acc_ref[...] = jnp.zeros(acc_ref.shape, F32)

    acc_ref[...] += dot(a_ref[...], b_ref[...])

    @pl.when(k == nk - 1)
    def _():
        epilogue(acc_ref[...], o_ref, *extra)


def _store_epilogue(acc, o_ref):
    o_ref[...] = acc.astype(o_ref.dtype)


def _mm(a, b, *, nt=False, tm, tn, tk, out_shape, out_block=None, out_index=None,
        extras=(), epilogue=_store_epilogue, n_outer=False, name="mm"):
    m, kdim = a.shape
    n = b.shape[0] if nt else b.shape[1]
    tm, tn, tk = _tile(tm, m), _tile(tn, n), _tile(tk, kdim)
    nm, nn, nk = m // tm, n // tn, kdim // tk
    if n_outer:
        grid = (nn, nm, nk)
        ij = lambda g0, g1: (g1, g0)
    else:
        grid = (nm, nn, nk)
        ij = lambda g0, g1: (g0, g1)

    def wrap(fn):
        return lambda g0, g1, k: fn(*ij(g0, g1), k)

    a_spec = pl.BlockSpec((tm, tk), wrap(lambda i, j, k: (i, k)))
    if nt:
        b_spec = pl.BlockSpec((tn, tk), wrap(lambda i, j, k: (j, k)))
    else:
        b_spec = pl.BlockSpec((tk, tn), wrap(lambda i, j, k: (k, j)))
    extra_specs = [pl.BlockSpec(blk, wrap(lambda i, j, k, f=f: f(i, j))) for _, blk, f in extras]
    if out_block is None:
        out_block, out_index = (tm, tn), (lambda i, j: (i, j))
    o_spec = pl.BlockSpec(out_block, wrap(lambda i, j, k: out_index(i, j)))
    scratch = [pltpu.VMEM((tm, tn), F32)] if nk > 1 else []
    body = functools.partial(_mm_body, nk=nk, nt=nt, n_extra=len(extras), epilogue=epilogue)
    return pl.pallas_call(
        body, out_shape=out_shape, grid=grid,
        in_specs=[a_spec, b_spec] + extra_specs, out_specs=o_spec,
        scratch_shapes=scratch, name=name,
        compiler_params=_cparams(("parallel", "parallel", "arbitrary")),
    )(a, b, *[e[0] for e in extras])


def _resid_epilogue(acc, o_ref, x_ref, g_ref):
    o_ref[...] = x_ref[...] + g_ref[0] * acc


def _resid_t_epilogue(acc, o_ref, x_ref, g_ref):
    o_ref[...] = x_ref[...] + g_ref[0] * acc.T


def _dot_split(a, b):
    a_hi, a_lo = _split2(a)
    b_hi, b_lo = _split2(b)
    return _dot(a_hi, b_hi) + _dot(a_hi, b_lo) + _dot(a_lo, b_hi)


def _ada_body(c_ref, w1_ref, w2_ref, b_ref, tab_ref, o_ref):
    c = c_ref[...]
    t = c * jax.nn.sigmoid(c)
    t1 = _dot_split(t, w1_ref[...])
    t0 = _dot_split(t1, w2_ref[...]) + b_ref[...]
    for l in range(o_ref.shape[0]):
        o_ref[l] = t0 + tab_ref[l:l + 1, :]


def _ada(c, w1, w2, b, table):
    bsz, d = c.shape
    depth, n = table.shape
    rank = w1.shape[1]
    bp = 8 * ((bsz + 7) // 8)
    cp = jnp.zeros((bp, d), F32).at[:bsz].set(c)
    tn = _tile(2048, d)
    out = pl.pallas_call(
        _ada_body, out_shape=jax.ShapeDtypeStruct((depth, bp, n), F32), grid=(n // tn,),
        in_specs=[pl.BlockSpec((bp, d), lambda j: (0, 0)),
                  pl.BlockSpec((d, rank), lambda j: (0, 0)),
                  pl.BlockSpec((rank, tn), lambda j: (0, j)),
                  pl.BlockSpec((1, tn), lambda j: (0, j)),
                  pl.BlockSpec((depth, tn), lambda j: (0, j))],
        out_specs=pl.BlockSpec((depth, bp, tn), lambda j: (0, 0, j)),
        name="ada", compiler_params=_cparams(("parallel",)),
    )(cp, w1, w2, b.reshape(1, n), table)
    return out[:, :bsz].reshape(depth * bsz, 1, n)


def _rms(x, g):
    return x * lax.rsqrt(jnp.mean(x * x, axis=-1, keepdims=True) + NORM_EPS) * g


def _norm_mod_body(x_ref, g_ref, sc_ref, sh_ref, o_ref):
    y = _rms(x_ref[...], g_ref[...])
    o_ref[...] = (y * (1.0 + sc_ref[0]) + sh_ref[0]).astype(o_ref.dtype)


def _norm_body(x_ref, g_ref, o_ref):
    o_ref[...] = _rms(x_ref[...], g_ref[...]).astype(o_ref.dtype)


def _norm_mod(x, g, mod, row0, sh_col, sc_col, seq):
    t, d = x.shape
    tm = _tile(512, seq)
    return pl.pallas_call(
        _norm_mod_body, out_shape=jax.ShapeDtypeStruct((t, d), BF16), grid=(t // tm,),
        in_specs=[pl.BlockSpec((tm, d), lambda i: (i, 0)),
                  pl.BlockSpec((1, d), lambda i: (0, 0)),
                  pl.BlockSpec((1, 1, d), lambda i: (row0 + (i * tm) // seq, 0, sc_col)),
                  pl.BlockSpec((1, 1, d), lambda i: (row0 + (i * tm) // seq, 0, sh_col))],
        out_specs=pl.BlockSpec((tm, d), lambda i: (i, 0)),
        name="norm_mod", compiler_params=_cparams(("parallel",)),
    )(x, g.reshape(1, d), mod, mod)


def _final_norm(x, g):
    t, d = x.shape
    tm = _tile(512, t)
    return pl.pallas_call(
        _norm_body, out_shape=jax.ShapeDtypeStruct((t, d), F32), grid=(t // tm,),
        in_specs=[pl.BlockSpec((tm, d), lambda i: (i, 0)),
                  pl.BlockSpec((1, d), lambda i: (0, 0))],
        out_specs=pl.BlockSpec((tm, d), lambda i: (i, 0)),
        name="final_norm", compiler_params=_cparams(("parallel",)),
    )(x, g.reshape(1, d))


def _attn_body(it_ref, jt_ref, q_ref, k_ref, v_ref, lam_ref, g_ref, o_ref,
               q_s, m_s, l_s, a_s, *, tq, tk, hb, lam_init):
    i = it_ref[pl.program_id(2)]
    j = jt_ref[pl.program_id(2)]
    vd = 2 * ATT_HD

    @pl.when(j == 0)
    def _():
        q = q_ref[...] * (ATT_HD ** -0.5)
        lane = lax.broadcasted_iota(jnp.int32, (tq, vd), 1)
        for hh in range(hb):
            qh = q[:, hh * vd:(hh + 1) * vd]
            q_s[hh, :tq, :] = jnp.where(lane < ATT_HD, qh, 0).astype(BF16)
            q_s[hh, tq:, :] = jnp.where(lane >= ATT_HD, qh, 0).astype(BF16)
        m_s[...] = jnp.full(m_s.shape, NEG, F32)
        l_s[...] = jnp.zeros(l_s.shape, F32)
        a_s[...] = jnp.zeros(a_s.shape, F32)

    def step(masked):
        if masked:
            krow = lax.broadcasted_iota(jnp.int32, (tk, 2 * tq), 0)
            qcol = lax.broadcasted_iota(jnp.int32, (tk, 2 * tq), 1)
            causal = krow <= jnp.where(qcol >= tq, qcol - tq, qcol)
        for hh in range(hb):
            k = k_ref[:, hh * vd:(hh + 1) * vd]
            v = v_ref[:, hh * vd:(hh + 1) * vd]
            s = _dot_nt(k, q_s[hh])
            if masked:
                s = jnp.where(causal, s, NEG)
            m_old = m_s[hh]
            m_new = jnp.maximum(m_old, jnp.max(s, axis=0, keepdims=True))
            alpha = jnp.exp(m_old - m_new)
            p = jnp.exp(s - m_new)
            l_s[hh] = alpha * l_s[hh] + jnp.sum(p, axis=0, keepdims=True)
            pv = lax.dot_general(v, p.astype(BF16), (((0,), (0,)), ((), ())),
                                 preferred_element_type=F32)
            a_s[hh] = alpha * a_s[hh] + pv
            m_s[hh] = m_new

    @pl.when(j < i)
    def _():
        step(False)

    @pl.when(j == i)
    def _():
        step(True)
        lp = lam_ref[...]
        lam = (jnp.exp(jnp.sum(lp[0:1] * lp[1:2], axis=-1, keepdims=True))
               - jnp.exp(jnp.sum(lp[2:3] * lp[3:4], axis=-1, keepdims=True)) + lam_init)
        for hh in range(hb):
            w = a_s[hh] / l_s[hh]
            o = w[:, :tq] - lam * w[:, tq:]
            o = o * lax.rsqrt(jnp.mean(o * o, axis=0, keepdims=True) + NORM_EPS) * g_ref[...]
            o_ref[:, hh * vd:(hh + 1) * vd] = (o * (1.0 - lam_init)).T.astype(o_ref.dtype)


def _diff_attention(qkv, lam_p, subln_g, lam_init, bsz, seq, heads):
    t = qkv.shape[0]
    vd = 2 * ATT_HD
    hb = 2 if heads % 2 == 0 else 1
    hblk = heads // hb
    tq = tk = _tile(512, seq)
    nq = seq // tq
    pairs = [(i, j) for i in range(nq) for j in range(i + 1)]
    i_tab = jnp.asarray([p[0] for p in pairs], jnp.int32)
    j_tab = jnp.asarray([p[1] for p in pairs], jnp.int32)
    body = functools.partial(_attn_body, tq=tq, tk=tk, hb=hb, lam_init=lam_init)
    grid_spec = pltpu.PrefetchScalarGridSpec(
        num_scalar_prefetch=2, grid=(bsz, hblk, len(pairs)),
        in_specs=[pl.BlockSpec((tq, hb * vd), lambda b, h, p, it, jt: (b * nq + it[p], h)),
                  pl.BlockSpec((tk, hb * vd), lambda b, h, p, it, jt: (b * nq + jt[p], hblk + h)),
                  pl.BlockSpec((tk, hb * vd), lambda b, h, p, it, jt: (b * nq + jt[p], 2 * hblk + h)),
                  pl.BlockSpec((4, ATT_HD), lambda b, h, p, it, jt: (0, 0)),
                  pl.BlockSpec((vd, 1), lambda b, h, p, it, jt: (0, 0))],
        out_specs=pl.BlockSpec((tq, hb * vd), lambda b, h, p, it, jt: (b * nq + it[p], h)),
        scratch_shapes=[pltpu.VMEM((hb, 2 * tq, vd), BF16), pltpu.VMEM((hb, 1, 2 * tq), F32),
                        pltpu.VMEM((hb, 1, 2 * tq), F32), pltpu.VMEM((hb, vd, 2 * tq), F32)])
    return pl.pallas_call(
        body, out_shape=jax.ShapeDtypeStruct((t, heads * vd), BF16), grid_spec=grid_spec,
        name="diff_attn",
        compiler_params=_cparams(("parallel", "parallel", "arbitrary")),
    )(i_tab, j_tab, qkv, qkv, qkv, lam_p, subln_g.reshape(vd, 1))


def _sgu_body(u_ref, v_ref, lng_ref, lnb_ref, w_ref, bias_ref, o_ref, *, groups, nchunk):
    u = jax.nn.gelu(u_ref[...])
    v = jax.nn.gelu(v_ref[...])
    mu = jnp.mean(v, axis=-1, keepdims=True)
    vc = v - mu
    var = jnp.mean(vc * vc, axis=-1, keepdims=True)
    vn = (vc * lax.rsqrt(var + NORM_EPS) * lng_ref[...] + lnb_ref[...]).astype(BF16)
    ri = lax.broadcasted_iota(jnp.int32, (CHUNK, CHUNK), 0)
    ci = lax.broadcasted_iota(jnp.int32, (CHUNK, CHUNK), 1)
    causal = ci <= ri
    for g in range(groups):
        cs = slice(g * LANES, (g + 1) * LANES)
        wg = jnp.where(causal, w_ref[g], 0.0).astype(BF16)
        bias = bias_ref[:, cs]
        for c in range(nchunk):
            rs = slice(c * CHUNK, (c + 1) * CHUNK)
            mixed = _dot(wg, vn[rs, cs]) + bias
            o_ref[rs, cs] = (u[rs, cs] * mixed).astype(o_ref.dtype)


def _sgu(ug, ln_g, ln_b, w_s, b_s):
    t, w2 = ug.shape
    w = w2 // 2
    groups = w // LANES
    ts = _tile(256, t)
    bias = jnp.repeat(b_s.T, LANES, axis=1)
    body = functools.partial(_sgu_body, groups=groups, nchunk=ts // CHUNK)
    return pl.pallas_call(
        body, out_shape=jax.ShapeDtypeStruct((t, w), BF16), grid=(t // ts,),
        in_specs=[pl.BlockSpec((ts, w), lambda i: (i, 0)),
                  pl.BlockSpec((ts, w), lambda i: (i, 1)),
                  pl.BlockSpec((1, w), lambda i: (0, 0)),
                  pl.BlockSpec((1, w), lambda i: (0, 0)),
                  pl.BlockSpec((groups, CHUNK, CHUNK), lambda i: (0, 0, 0)),
                  pl.BlockSpec((CHUNK, w), lambda i: (0, 0))],
        out_specs=pl.BlockSpec((ts, w), lambda i: (i, 0)),
        name="sgu", compiler_params=_cparams(("parallel",)),
    )(ug, ug, ln_g.reshape(1, w), ln_b.reshape(1, w), w_s, bias)


def _conv_body(x_ref, w_ref, b_ref, o_ref, prev_s):
    @pl.when(pl.program_id(2) == 0)
    def _():
        prev_s[...] = jnp.zeros(prev_s.shape, F32)

    x = x_ref[...]
    prev = prev_s[...]
    row = lax.broadcasted_iota(jnp.int32, x.shape, 0)
    acc = x * w_ref[CONV_K - 1:CONV_K, :] + b_ref[...]
    for k in range(1, CONV_K):
        xs = jnp.where(row < k, pltpu.roll(prev, k, 0), pltpu.roll(x, k, 0))
        acc = acc + xs * w_ref[CONV_K - 1 - k:CONV_K - k, :]
    o_ref[...] = acc * jax.nn.sigmoid(acc)
    prev_s[...] = x


def _conv_silu(xbc, w, b, bsz, seq):
    t, c = xbc.shape
    ts = _tile(512, seq)
    tc = _tile(2048, c)
    ns = seq // ts
    return pl.pallas_call(
        _conv_body, out_shape=jax.ShapeDtypeStruct((t, c), F32), grid=(c // tc, bsz, ns),
        in_specs=[pl.BlockSpec((ts, tc), lambda ci, bi, si: (bi * ns + si, ci)),
                  pl.BlockSpec((CONV_K, tc), lambda ci, bi, si: (0, ci)),
                  pl.BlockSpec((1, tc), lambda ci, bi, si: (0, ci))],
        out_specs=pl.BlockSpec((ts, tc), lambda ci, bi, si: (bi * ns + si, ci)),
        scratch_shapes=[pltpu.VMEM((ts, tc), F32)],
        name="conv_silu", compiler_params=_cparams(("parallel", "parallel", "arbitrary")),
    )(xbc, w, b.reshape(1, c))


def _softplus(x):
    return jnp.maximum(x, 0.0) + jnp.log1p(jnp.exp(-jnp.abs(x)))


def _ssd_body(x_ref, b_ref, c_ref, z_ref, dta_ref, dtb_ref, ba_ref, bb_ref, ala_ref, alb_ref,
              d_ref, ng_ref, o_ref, state_s, y_s, *, hpg):
    n = pl.program_id(2)
    L = CHUNK
    gw = hpg * SSD_HD

    @pl.when(n == 0)
    def _():
        state_s[...] = jnp.zeros(state_s.shape, F32)

    dt = _softplus(dta_ref[0] + ba_ref[0])
    dt_t = _softplus(dtb_ref[0] + bb_ref[0])
    da = dt * (-jnp.exp(ala_ref[0]))
    da_t = dt_t * (-jnp.exp(alb_ref[0]))
    ri = lax.broadcasted_iota(jnp.int32, (L, L), 0)
    ci = lax.broadcasted_iota(jnp.int32, (L, L), 1)
    causal = ci <= ri
    tri = causal.astype(BF16)
    tri_t = (ri <= ci).astype(BF16)
    acs = sum(_dot(tri, p) for p in _split3(da))
    acs_t = sum(_dot(p, tri_t) for p in _split3(da_t))
    acs_last = acs[L - 1:L, :]

    head_of_col = lax.broadcasted_iota(jnp.int32, (hpg, gw), 1) // SSD_HD
    expand = (head_of_col == lax.broadcasted_iota(jnp.int32, (hpg, gw), 0)).astype(BF16)

    def widen(v):
        return sum(_dot(p, expand) for p in _split2(v))

    dt_e = widen(dt)
    dec_e = widen(jnp.exp(acs))
    tail_e = widen(jnp.exp(acs_last - acs))

    x = x_ref[...]
    xdt = x * dt_e
    xdt_b = xdt.astype(BF16)
    bm = b_ref[...]
    cm_b = c_ref[...].astype(BF16)
    bm_b = bm.astype(BF16)
    cb = _dot_nt(cm_b, bm_b)
    state = state_s[...]
    y_s[...] = _dot(cm_b, state.astype(BF16)) * dec_e

    lane = lax.broadcasted_iota(jnp.int32, (L, LANES), 1)
    for pair in range(hpg // 2):
        cs = slice(pair * LANES, (pair + 1) * LANES)
        xp = xdt_b[:, cs]
        acc = None
        for half in range(2):
            r = 2 * pair + half
            seg = acs[:, r:r + 1] - acs_t[r:r + 1, :]
            decay = jnp.exp(jnp.where(causal, seg, NEG))
            mm = (cb * decay).astype(BF16)
            keep = (lane < SSD_HD) if half == 0 else (lane >= SSD_HD)
            part = _dot(mm, jnp.where(keep, xp, 0))
            acc = part if acc is None else acc + part
        y_s[:, cs] += acc

    state_s[...] = state * dec_e[L - 1:L, :] + _dot(bm.T.astype(BF16), (tail_e * xdt).astype(BF16))

    y = y_s[...] + x * d_ref[...]
    z = z_ref[...].astype(F32)
    y = y * (z * jax.nn.sigmoid(z))
    o_ref[...] = _rms(y, ng_ref[...]).astype(o_ref.dtype)


def _ssd_scan(xc, z, dt_raw, dt_bias, a_log, d_skip, norm_g, bsz, seq, d_inner):
    t = xc.shape[0]
    g, n_state = SSD_GROUPS, SSD_STATE
    heads = dt_raw.shape[1]
    hpg = heads // g
    gw = hpg * SSD_HD
    assert gw % LANES == 0 and hpg % 2 == 0
    nc = seq // CHUNK
    dta = dt_raw.reshape(t, g, hpg).transpose(1, 0, 2)
    dtb = dta.transpose(0, 2, 1)
    ba = dt_bias.reshape(g, 1, hpg)
    bb = dt_bias.reshape(g, hpg, 1)
    ala = a_log.reshape(g, 1, hpg)
    alb = a_log.reshape(g, hpg, 1)
    d_e = jnp.repeat(d_skip, SSD_HD).reshape(1, d_inner)
    xcol = gw // LANES
    b0 = d_inner // n_state
    c0 = (d_inner + g * n_state) // n_state
    row = lambda bi, ni: bi * nc + ni
    body = functools.partial(_ssd_body, hpg=hpg)
    return pl.pallas_call(
        body, out_shape=jax.ShapeDtypeStruct((t, d_inner), BF16), grid=(bsz, g, nc),
        in_specs=[pl.BlockSpec((CHUNK, gw), lambda bi, gi, ni: (row(bi, ni), gi)),
                  pl.BlockSpec((CHUNK, n_state), lambda bi, gi, ni: (row(bi, ni), b0 + gi)),
                  pl.BlockSpec((CHUNK, n_state), lambda bi, gi, ni: (row(bi, ni), c0 + gi)),
                  pl.BlockSpec((CHUNK, gw), lambda bi, gi, ni: (row(bi, ni), gi)),
                  pl.BlockSpec((1, CHUNK, hpg), lambda bi, gi, ni: (gi, row(bi, ni), 0)),
                  pl.BlockSpec((1, hpg, CHUNK), lambda bi, gi, ni: (gi, 0, row(bi, ni))),
                  pl.BlockSpec((1, 1, hpg), lambda bi, gi, ni: (gi, 0, 0)),
                  pl.BlockSpec((1, hpg, 1), lambda bi, gi, ni: (gi, 0, 0)),
                  pl.BlockSpec((1, 1, hpg), lambda bi, gi, ni: (gi, 0, 0)),
                  pl.BlockSpec((1, hpg, 1), lambda bi, gi, ni: (gi, 0, 0)),
                  pl.BlockSpec((1, gw), lambda bi, gi, ni: (0, gi)),
                  pl.BlockSpec((1, gw), lambda bi, gi, ni: (0, gi))],
        out_specs=pl.BlockSpec((CHUNK, gw), lambda bi, gi, ni: (row(bi, ni), gi)),
        scratch_shapes=[pltpu.VMEM((n_state, gw), F32), pltpu.VMEM((CHUNK, gw), F32)],
        name="ssd_scan", compiler_params=_cparams(("parallel", "parallel", "arbitrary")),
    )(xc, xc, xc, z, dta, dtb, ba, bb, ala, alb, d_e, norm_g.reshape(1, d_inner))


def _topk_rank(s, k_top, exact):
    n = s.shape[0]
    iota = lax.broadcasted_iota(jnp.int32, s.shape, 0).astype(F32)
    rank = jnp.full(s.shape, float(k_top), F32)
    vals = []
    for k in range(k_top):
        m = jnp.max(s, axis=0, keepdims=True)
        hit = s == m
        if exact:
            idx = jnp.min(jnp.where(hit, iota, float(n)), axis=0, keepdims=True)
            hit = iota == idx
        rank = jnp.where(hit, float(k), rank)
        s = jnp.where(hit, -jnp.inf, s)
        vals.append(m)
    return rank, jnp.concatenate(vals, axis=0)


_CANDS = [(a, b) for a in range(PEER_TOPK) for b in range(PEER_TOPK // (a + 1))]
N_CAND = 64


def _cand_tables():
    pa = np.zeros((N_CAND, PEER_TOPK), np.float32)
    pb = np.zeros((N_CAND, PEER_TOPK), np.float32)
    for r, (a, b) in enumerate(_CANDS):
        pa[r, a] = 1.0
        pb[r, b] = 1.0
    return jnp.asarray(pa, BF16), jnp.asarray(pb, BF16), jnp.asarray(pa.T, BF16)


def _pick_rows(onehot, v):
    hi, mid, lo = _split3(v)
    return (_dot(onehot, hi) + _dot(onehot, mid)) + _dot(onehot, lo)


def _route_compute(s1, s2, pa, pb, ga, exact):
    kt = PEER_TOPK
    r1, v1 = _topk_rank(s1, kt, exact)
    r2, v2 = _topk_rank(s2, kt, exact)
    c1 = _pick_rows(pa, v1)
    c2 = _pick_rows(pb, v2)
    row = lax.broadcasted_iota(jnp.int32, c1.shape, 0)
    cand = jnp.where(row < len(_CANDS), c1 + c2, -jnp.inf)
    crank, _ = _topk_rank(cand, kt, exact)
    sel = crank < float(kt)
    gates = jnp.where(sel, jnp.exp(c1 - v1[0:1]) * jnp.exp(c2 - v2[0:1]), 0.0)
    zsum = jnp.sum(gates, axis=0, keepdims=True)
    count = _dot(ga, sel.astype(F32).astype(BF16))
    lk = jnp.zeros(s1.shape, F32)
    for a in range(kt):
        lk = jnp.where(r1 == float(a), count[a:a + 1], lk)
    e1 = jnp.exp(s1 - v1[0:1])
    e2 = jnp.exp(s2 - v2[0:1]) / zsum * 0.5
    nsel = sum(jnp.sum((r < float(kt)).astype(F32), axis=0, keepdims=True) for r in (r1, r2, crank))
    return (lk, r2, e1, e2), nsel


def _route_body(q_ref, k_ref, pa_ref, pb_ref, ga_ref, lk_ref, r2_ref, e1_ref, e2_ref):
    q = q_ref[...]
    s1 = _dot_nt(k_ref[0, 0], q[:, :N_KEYS])
    s2 = _dot_nt(k_ref[0, 1], q[:, N_KEYS:])
    tabs = (pa_ref[...], pb_ref[...], ga_ref[...])

    def store(outs):
        for ref, val in zip((lk_ref, r2_ref, e1_ref, e2_ref), outs):
            ref[0] = val.astype(ref.dtype)

    outs, nsel = _route_compute(s1, s2, *tabs, exact=False)
    tied = jnp.max(jnp.abs(nsel - 3.0 * PEER_TOPK)) > 0.0

    @pl.when(jnp.logical_not(tied))
    def _():
        store(outs)

    @pl.when(tied)
    def _():
        store(_route_compute(s1, s2, *tabs, exact=True)[0])


def _peer_route(q, keys):
    t = q.shape[0]
    tt = _tile(512, t)
    f32_shp = jax.ShapeDtypeStruct((PEER_HEADS, N_KEYS, t), F32)
    bf16_shp = jax.ShapeDtypeStruct((PEER_HEADS, N_KEYS, t), BF16)
    ospec = pl.BlockSpec((1, N_KEYS, tt), lambda i, h: (h, 0, i))
    tab_spec = lambda shape: pl.BlockSpec(shape, lambda i, h: (0, 0))
    pa, pb, ga = _cand_tables()
    return pl.pallas_call(
        _route_body, out_shape=(f32_shp, bf16_shp, f32_shp, bf16_shp), grid=(t // tt, PEER_HEADS),
        in_specs=[pl.BlockSpec((tt, 2 * N_KEYS), lambda i, h: (i, h)),
                  pl.BlockSpec((1, 2, N_KEYS, N_KEYS), lambda i, h: (h, 0, 0, 0)),
                  tab_spec(pa.shape), tab_spec(pb.shape), tab_spec(ga.shape)],
        out_specs=(ospec, ospec, ospec, ospec),
        name="peer_route", compiler_params=_cparams(("parallel", "parallel")),
    )(q, keys, pa, pb, ga)


def _gelu_tanh(x, half_scale):
    c0 = math.sqrt(2.0 / math.pi)
    inner = x * (c0 + (c0 * 0.044715) * (x * x))
    return (x * half_scale) * (1.0 + jnp.tanh(inner))


def _peer_gate_epilogue(acc, o_ref, lk_ref, e1_ref, r2_ref, e2_ref):
    for ii in range(acc.shape[0] // N_KEYS):
        gate = None
        for h in range(PEER_HEADS):
            lk = lk_ref[h, ii:ii + 1, :].astype(BF16)
            e1 = e1_ref[h, ii:ii + 1, :].astype(BF16)
            term = e1 * jnp.where(r2_ref[h] < lk, e2_ref[h], jnp.zeros((), BF16))
            gate = term if gate is None else gate + term
        w = _gelu_tanh(acc[ii * N_KEYS:(ii + 1) * N_KEYS, :], gate.astype(F32))
        o_ref[ii * N_KEYS:(ii + 1) * N_KEYS, :] = w.astype(o_ref.dtype)


def kernel(x, c, ada_w1, ada_w2, ada_b, ada_table, norm_mix, norm_ffn, norm_final, hyb_w_in, hyb_w_out, diff_lam, diff_subln, sgu_ln_g, sgu_ln_b, sgu_w_s, sgu_b_s, ssd_w_in, ssd_conv_w, ssd_conv_b, ssd_dt_bias, ssd_a_log, ssd_d, ssd_norm, ssd_w_out, peer_w_q, peer_keys, peer_u, peer_v):
    bsz, seq, d = x.shape
    depth = ada_table.shape[0]
    t = bsz * seq
    n_experts = peer_u.shape[1]
    d_inner = ssd_norm.shape[1]
    ssd_heads = ssd_a_log.shape[1]
    conv_dim = ssd_conv_b.shape[1]
    sgu_w = sgu_ln_g.shape[1]
    att_w = hyb_w_out.shape[1] - sgu_w
    att_heads = att_w // (2 * ATT_HD)

    mod = _ada(c, ada_w1, ada_w2, ada_b, ada_table)
    xt = x.reshape(t, d)

    def gate_spec(l, col, tn):
        per = d // tn
        return lambda tm: (mod, (1, 1, tn), lambda i, j: (l * bsz + (i * tm) // seq, 0, col * per + j))

    for l in range(depth):
        h = _norm_mod(xt, norm_mix[l], mod, l * bsz, 0, 1, seq)
        if l % 2 == 0:
            e = l // 2
            lam_init = 0.8 - 0.6 * math.exp(-0.3 * l)
            w_in = hyb_w_in[e]
            qkv = _mm(h, w_in[:, :3 * att_w].astype(BF16), tm=1024, tn=1024, tk=d,
                      out_shape=jax.ShapeDtypeStruct((t, 3 * att_w), BF16), name="hyb_qkv")
            ug = _mm(h, w_in[:, 3 * att_w:].astype(BF16), tm=1024, tn=1024, tk=d,
                     out_shape=jax.ShapeDtypeStruct((t, 2 * sgu_w), F32), name="hyb_ug")
            a_out = _diff_attention(qkv, diff_lam[e], diff_subln[e], lam_init, bsz, seq, att_heads)
            s_out = _sgu(ug, sgu_ln_g[e], sgu_ln_b[e], sgu_w_s[e], sgu_b_s[e])
            y_in = jnp.concatenate([a_out, s_out], axis=-1)
            w_out = hyb_w_out[e].astype(BF16)
        else:
            o = l // 2
            w_in = ssd_w_in[o]
            z = _mm(h, w_in[:, :d_inner].astype(BF16), tm=1024, tn=1024, tk=d,
                    out_shape=jax.ShapeDtypeStruct((t, d_inner), F32), name="ssd_z")
            xbc = _mm(h, w_in[:, d_inner:d_inner + conv_dim].astype(BF16), tm=1024, tn=1024, tk=d,
                      out_shape=jax.ShapeDtypeStruct((t, conv_dim), F32), name="ssd_xbc")
            dt_raw = _mm(h, w_in[:, d_inner + conv_dim:].astype(BF16), tm=1024, tn=ssd_heads, tk=d,
                         out_shape=jax.ShapeDtypeStruct((t, ssd_heads), F32), name="ssd_dt")
            xc = _conv_silu(xbc, ssd_conv_w[o], ssd_conv_b[o], bsz, seq)
            y_in = _ssd_scan(xc, z, dt_raw, ssd_dt_bias[o], ssd_a_log[o], ssd_d[o], ssd_norm[o],
                             bsz, seq, d_inner)
            w_out = ssd_w_out[o].astype(BF16)
        tm, tn = _tile(1024, seq), _tile(1024, d)
        xt = _mm(y_in, w_out, tm=tm, tn=tn, tk=2048,
                 out_shape=jax.ShapeDtypeStruct((t, d), F32),
                 extras=[(xt, (tm, tn), lambda i, j: (i, j)), gate_spec(l, 2, tn)(tm)],
                 epilogue=_resid_epilogue, name="mix_out")

        h = _norm_mod(xt, norm_ffn[l], mod, l * bsz, 3, 4, seq)
        q = _mm(h, peer_w_q[l].astype(BF16), tm=1024, tn=1024, tk=d,
                out_shape=jax.ShapeDtypeStruct((t, peer_w_q.shape[2]), BF16), name="peer_q")
        lk, r2, e1, e2 = _peer_route(q, peer_keys[l].astype(BF16))
        te, tt = _tile(1024, n_experts), _tile(512, t)
        ni1 = te // N_KEYS
        head_blk = (PEER_HEADS, ni1, tt)
        full_blk = (PEER_HEADS, N_KEYS, tt)
        wt = _mm(peer_u[l].astype(BF16), h, nt=True, tm=te, tn=tt, tk=d, n_outer=True,
                 out_shape=jax.ShapeDtypeStruct((n_experts, t), BF16),
                 extras=[(lk, head_blk, lambda i, j: (0, i, j)),
                         (e1, head_blk, lambda i, j: (0, i, j)),
                         (r2, full_blk, lambda i, j: (0, 0, j)),
                         (e2, full_blk, lambda i, j: (0, 0, j))],
                 epilogue=_peer_gate_epilogue, name="peer_gate")
        td, tt = _tile(1024, d), _tile(1024, seq)
        xt = _mm(peer_v[l].astype(BF16).T, wt, tm=td, tn=tt, tk=2048,
                 out_shape=jax.ShapeDtypeStruct((t, d), F32),
                 out_block=(tt, td), out_index=lambda i, j: (j, i),
                 extras=[(xt, (tt, td), lambda i, j: (j, i)),
                         (mod, (1, 1, td), lambda i, j, l=l, tt=tt, td=td:
                          (l * bsz + (j * tt) // seq, 0, 5 * (d // td) + i))],
                 epilogue=_resid_t_epilogue, name="peer_out")

    return _final_norm(xt, norm_final).reshape(bsz, seq, d)
```

```python
import functools
import math

import jax
import jax.numpy as jnp
import numpy as np
from jax import lax
from jax.experimental import pallas as pl
from jax.experimental.pallas import tpu as pltpu

F32 = jnp.float32
BF16 = jnp.bfloat16
NORM_EPS = 1e-6
LANES = 128
VMEM_LIMIT = 56 * 1024 * 1024
NEG = -1e30

ATT_HD = 64
SSD_HD = 64
SSD_GROUPS = 8
SSD_STATE = 128
CONV_K = 4
CHUNK = 128
PEER_HEADS = 8
N_KEYS = 128
PEER_TOPK = 16
ADA_N_MOD = 6


def _cparams(sem):
    return pltpu.CompilerParams(dimension_semantics=sem, vmem_limit_bytes=VMEM_LIMIT)


def _tile(pref, dim):
    t = min(pref, dim)
    while dim % t:
        t -= LANES
        assert t > 0, (pref, dim)
    return t


def _split2(a):
    hi = a.astype(BF16)
    lo = (a - hi.astype(F32)).astype(BF16)
    return hi, lo


def _split3(a):
    hi = a.astype(BF16)
    r = a - hi.astype(F32)
    mid = r.astype(BF16)
    lo = (r - mid.astype(F32)).astype(BF16)
    return hi, mid, lo


def _dot(a, b):
    return jnp.dot(a, b, preferred_element_type=F32)


def _dot_nt(a, b):
    return lax.dot_general(a, b, (((1,), (1,)), ((), ())), preferred_element_type=F32)


def _mm_body(*refs, nk, nt, n_extra, epilogue):
    a_ref, b_ref = refs[0], refs[1]
    extra = refs[2:2 + n_extra]
    o_ref = refs[2 + n_extra]
    dot = _dot_nt if nt else _dot
    if nk == 1:
        epilogue(dot(a_ref[...], b_ref[...]), o_ref, *extra)
        return
    acc_ref = refs[3 + n_extra]
    k = pl.program_id(2)

    @pl.when(k == 0)
    def _():
        acc_ref[...] = jnp.zeros(acc_ref.shape, F32)

    acc_ref[...] += dot(a_ref[...], b_ref[...])

    @pl.when(k == nk - 1)
    def _():
        epilogue(acc_ref[...], o_ref, *extra)


def _store_epilogue(acc, o_ref):
    o_ref[...] = acc.astype(o_ref.dtype)


def _mm(a, b, *, nt=False, tm, tn, tk, out_shape, out_block=None, out_index=None,
        extras=(), epilogue=_store_epilogue, n_outer=False, name="mm"):
    m, kdim = a.shape
    n = b.shape[0] if nt else b.shape[1]
    tm, tn, tk = _tile(tm, m), _tile(tn, n), _tile(tk, kdim)
    nm, nn, nk = m // tm, n // tn, kdim // tk
    if n_outer:
        grid = (nn, nm, nk)
        ij = lambda g0, g1: (g1, g0)
    else:
        grid = (nm, nn, nk)
        ij = lambda g0, g1: (g0, g1)

    def wrap(fn):
        return lambda g0, g1, k: fn(*ij(g0, g1), k)

    a_spec = pl.BlockSpec((tm, tk), wrap(lambda i, j, k: (i, k)))
    if nt:
        b_spec = pl.BlockSpec((tn, tk), wrap(lambda i, j, k: (j, k)))
    else:
        b_spec = pl.BlockSpec((tk, tn), wrap(lambda i, j, k: (k, j)))
    extra_specs = [pl.BlockSpec(blk, wrap(lambda i, j, k, f=f: f(i, j))) for _, blk, f in extras]
    if out_block is None:
        out_block, out_index = (tm, tn), (lambda i, j: (i, j))
    o_spec = pl.BlockSpec(out_block, wrap(lambda i, j, k: out_index(i, j)))
    scratch = [pltpu.VMEM((tm, tn), F32)] if nk > 1 else []
    body = functools.partial(_mm_body, nk=nk, nt=nt, n_extra=len(extras), epilogue=epilogue)
    return pl.pallas_call(
        body, out_shape=out_shape, grid=grid,
        in_specs=[a_spec, b_spec] + extra_specs, out_specs=o_spec,
        scratch_shapes=scratch, name=name,
        compiler_params=_cparams(("parallel", "parallel", "arbitrary")),
    )(a, b, *[e[0] for e in extras])


def _mm_w32_body(a_ref, b_ref, *refs, n_extra, epilogue):
    extra = refs[:n_extra]
    o_ref, bq_ref = refs[n_extra], refs[n_extra + 1]

    @pl.when(pl.program_id(1) == 0)
    def _():
        bq_ref[...] = b_ref[...].astype(BF16)

    epilogue(_dot(a_ref[...], bq_ref[...]), o_ref, *extra)


def _mm_w32(a, w, layer, col0, n, *, tm, tn, out_shape, extras=(), epilogue=_store_epilogue,
            name="mm_w32"):
    m, kdim = a.shape
    tm = _tile(tm, m)
    tn = min(tn, n)
    while n % tn or col0 % tn:
        tn -= LANES
        assert tn > 0, (n, col0)
    off = col0 // tn
    specs = [pl.BlockSpec((tm, kdim), lambda j, i: (i, 0)),
             pl.BlockSpec((None, kdim, tn), lambda j, i: (layer, 0, off + j))]
    specs += [pl.BlockSpec(blk, lambda j, i, f=f: f(i, j)) for _, blk, f in extras]
    body = functools.partial(_mm_w32_body, n_extra=len(extras), epilogue=epilogue)
    return pl.pallas_call(
        body, out_shape=out_shape, grid=(n // tn, m // tm),
        in_specs=specs, out_specs=pl.BlockSpec((tm, tn), lambda j, i: (i, j)),
        scratch_shapes=[pltpu.VMEM((kdim, tn), BF16)], name=name,
        compiler_params=_cparams(("parallel", "arbitrary")),
    )(a, w, *[e[0] for e in extras])


def _resid_epilogue(acc, o_ref, x_ref, g_ref):
    o_ref[...] = x_ref[...] + g_ref[0] * acc


def _resid_t_epilogue(acc, o_ref, x_ref, g_ref):
    o_ref[...] = x_ref[...] + g_ref[0] * acc.T


def _dot_split(a, b):
    a_hi, a_lo = _split2(a)
    b_hi, b_lo = _split2(b)
    return _dot(a_hi, b_hi) + _dot(a_hi, b_lo) + _dot(a_lo, b_hi)


def _ada_body(c_ref, w1_ref, w2_ref, b_ref, tab_ref, o_ref):
    c = c_ref[...]
    t = c * jax.nn.sigmoid(c)
    t1 = _dot_split(t, w1_ref[...])
    t0 = _dot_split(t1, w2_ref[...]) + b_ref[...]
    for l in range(o_ref.shape[0]):
        o_ref[l] = t0 + tab_ref[l:l + 1, :]


def _ada(c, w1, w2, b, table):
    bsz, d = c.shape
    depth, n = table.shape
    rank = w1.shape[1]
    bp = 8 * ((bsz + 7) // 8)
    cp = jnp.zeros((bp, d), F32).at[:bsz].set(c)
    tn = _tile(2048, d)
    out = pl.pallas_call(
        _ada_body, out_shape=jax.ShapeDtypeStruct((depth, bp, n), F32), grid=(n // tn,),
        in_specs=[pl.BlockSpec((bp, d), lambda j: (0, 0)),
                  pl.BlockSpec((d, rank), lambda j: (0, 0)),
                  pl.BlockSpec((rank, tn), lambda j: (0, j)),
                  pl.BlockSpec((1, tn), lambda j: (0, j)),
                  pl.BlockSpec((depth, tn), lambda j: (0, j))],
        out_specs=pl.BlockSpec((depth, bp, tn), lambda j: (0, 0, j)),
        name="ada", compiler_params=_cparams(("parallel",)),
    )(cp, w1, w2, b.reshape(1, n), table)
    return out[:, :bsz].reshape(depth * bsz, 1, n)


def _rms(x, g):
    return x * lax.rsqrt(jnp.mean(x * x, axis=-1, keepdims=True) + NORM_EPS) * g


def _norm_mod_body(x_ref, g_ref, sc_ref, sh_ref, o_ref):
    y = _rms(x_ref[...], g_ref[...])
    o_ref[...] = (y * (1.0 + sc_ref[0]) + sh_ref[0]).astype(o_ref.dtype)


def _norm_body(x_ref, g_ref, o_ref):
    o_ref[...] = _rms(x_ref[...], g_ref[...]).astype(o_ref.dtype)


def _norm_mod(x, g, mod, row0, sh_col, sc_col, seq):
    t, d = x.shape
    tm = _tile(512, seq)
    return pl.pallas_call(
        _norm_mod_body, out_shape=jax.ShapeDtypeStruct((t, d), BF16), grid=(t // tm,),
        in_specs=[pl.BlockSpec((tm, d), lambda i: (i, 0)),
                  pl.BlockSpec((1, d), lambda i: (0, 0)),
                  pl.BlockSpec((1, 1, d), lambda i: (row0 + (i * tm) // seq, 0, sc_col)),
                  pl.BlockSpec((1, 1, d), lambda i: (row0 + (i * tm) // seq, 0, sh_col))],
        out_specs=pl.BlockSpec((tm, d), lambda i: (i, 0)),
        name="norm_mod", compiler_params=_cparams(("parallel",)),
    )(x, g.reshape(1, d), mod, mod)


def _final_norm(x, g):
    t, d = x.shape
    tm = _tile(512, t)
    return pl.pallas_call(
        _norm_body, out_shape=jax.ShapeDtypeStruct((t, d), F32), grid=(t // tm,),
        in_specs=[pl.BlockSpec((tm, d), lambda i: (i, 0)),
                  pl.BlockSpec((1, d), lambda i: (0, 0))],
        out_specs=pl.BlockSpec((tm, d), lambda i: (i, 0)),
        name="final_norm", compiler_params=_cparams(("parallel",)),
    )(x, g.reshape(1, d))


def _attn_body(it_ref, jt_ref, q_ref, k_ref, v_ref, lam_ref, g_ref, o_ref,
               q_s, m_s, l_s, a_s, *, tq, tk, hb, lam_init):
    i = it_ref[pl.program_id(2)]
    j = jt_ref[pl.program_id(2)]
    vd = 2 * ATT_HD

    @pl.when(j == 0)
    def _():
        q = q_ref[...] * (ATT_HD ** -0.5)
        lane = lax.broadcasted_iota(jnp.int32, (tq, vd), 1)
        for hh in range(hb):
            qh = q[:, hh * vd:(hh + 1) * vd]
            q_s[hh, :tq, :] = jnp.where(lane < ATT_HD, qh, 0).astype(BF16)
            q_s[hh, tq:, :] = jnp.where(lane >= ATT_HD, qh, 0).astype(BF16)
        m_s[...] = jnp.full(m_s.shape, NEG, F32)
        l_s[...] = jnp.zeros(l_s.shape, F32)
        a_s[...] = jnp.zeros(a_s.shape, F32)

    def step(masked):
        if masked:
            krow = lax.broadcasted_iota(jnp.int32, (tk, 2 * tq), 0)
            qcol = lax.broadcasted_iota(jnp.int32, (tk, 2 * tq), 1)
            causal = krow <= jnp.where(qcol >= tq, qcol - tq, qcol)
        for hh in range(hb):
            k = k_ref[:, hh * vd:(hh + 1) * vd]
            v = v_ref[:, hh * vd:(hh + 1) * vd]
            s = _dot_nt(k, q_s[hh])
            if masked:
                s = jnp.where(causal, s, NEG)
            m_old = m_s[hh]
            m_new = jnp.maximum(m_old, jnp.max(s, axis=0, keepdims=True))
            alpha = jnp.exp(m_old - m_new)
            p = jnp.exp(s - m_new)
            l_s[hh] = alpha * l_s[hh] + jnp.sum(p, axis=0, keepdims=True)
            pv = lax.dot_general(v, p.astype(BF16), (((0,), (0,)), ((), ())),
                                 preferred_element_type=F32)
            a_s[hh] = alpha * a_s[hh] + pv
            m_s[hh] = m_new

    @pl.when(j < i)
    def _():
        step(False)

    @pl.when(j == i)
    def _():
        step(True)
        lp = lam_ref[...]
        lam = (jnp.exp(jnp.sum(lp[0:1] * lp[1:2], axis=-1, keepdims=True))
               - jnp.exp(jnp.sum(lp[2:3] * lp[3:4], axis=-1, keepdims=True)) + lam_init)
        for hh in range(hb):
            w = a_s[hh] / l_s[hh]
            o = w[:, :tq] - lam * w[:, tq:]
            o = o * lax.rsqrt(jnp.mean(o * o, axis=0, keepdims=True) + NORM_EPS) * g_ref[...]
            o_ref[:, hh * vd:(hh + 1) * vd] = (o * (1.0 - lam_init)).T.astype(o_ref.dtype)


def _diff_attention(qkv, lam_p, subln_g, lam_init, bsz, seq, heads):
    t = qkv.shape[0]
    vd = 2 * ATT_HD
    hb = 2 if heads % 2 == 0 else 1
    hblk = heads // hb
    tq = tk = _tile(512, seq)
    nq = seq // tq
    pairs = [(i, j) for i in range(nq) for j in range(i + 1)]
    i_tab = jnp.asarray([p[0] for p in pairs], jnp.int32)
    j_tab = jnp.asarray([p[1] for p in pairs], jnp.int32)
    body = functools.partial(_attn_body, tq=tq, tk=tk, hb=hb, lam_init=lam_init)
    grid_spec = pltpu.PrefetchScalarGridSpec(
        num_scalar_prefetch=2, grid=(bsz, hblk, len(pairs)),
        in_specs=[pl.BlockSpec((tq, hb * vd), lambda b, h, p, it, jt: (b * nq + it[p], h)),
                  pl.BlockSpec((tk, hb * vd), lambda b, h, p, it, jt: (b * nq + jt[p], hblk + h)),
                  pl.BlockSpec((tk, hb * vd), lambda b, h, p, it, jt: (b * nq + jt[p], 2 * hblk + h)),
                  pl.BlockSpec((4, ATT_HD), lambda b, h, p, it, jt: (0, 0)),
                  pl.BlockSpec((vd, 1), lambda b, h, p, it, jt: (0, 0))],
        out_specs=pl.BlockSpec((tq, hb * vd), lambda b, h, p, it, jt: (b * nq + it[p], h)),
        scratch_shapes=[pltpu.VMEM((hb, 2 * tq, vd), BF16), pltpu.VMEM((hb, 1, 2 * tq), F32),
                        pltpu.VMEM((hb, 1, 2 * tq), F32), pltpu.VMEM((hb, vd, 2 * tq), F32)])
    return pl.pallas_call(
        body, out_shape=jax.ShapeDtypeStruct((t, heads * vd), BF16), grid_spec=grid_spec,
        name="diff_attn",
        compiler_params=_cparams(("parallel", "parallel", "arbitrary")),
    )(i_tab, j_tab, qkv, qkv, qkv, lam_p, subln_g.reshape(vd, 1))


def _sgu_body(u_ref, v_ref, lng_ref, lnb_ref, w_ref, bias_ref, o_ref, *, groups, nchunk):
    u = jax.nn.gelu(u_ref[...])
    v = jax.nn.gelu(v_ref[...])
    mu = jnp.mean(v, axis=-1, keepdims=True)
    vc = v - mu
    var = jnp.mean(vc * vc, axis=-1, keepdims=True)
    vn = (vc * lax.rsqrt(var + NORM_EPS) * lng_ref[...] + lnb_ref[...]).astype(BF16)
    ri = lax.broadcasted_iota(jnp.int32, (CHUNK, CHUNK), 0)
    ci = lax.broadcasted_iota(jnp.int32, (CHUNK, CHUNK), 1)
    causal = ci <= ri
    for g in range(groups):
        cs = slice(g * LANES, (g + 1) * LANES)
        wg = jnp.where(causal, w_ref[g], 0.0).astype(BF16)
        bias = bias_ref[:, cs]
        for c in range(nchunk):
            rs = slice(c * CHUNK, (c + 1) * CHUNK)
            mixed = _dot(wg, vn[rs, cs]) + bias
            o_ref[rs, cs] = (u[rs, cs] * mixed).astype(o_ref.dtype)


def _sgu(ug, ln_g, ln_b, w_s, b_s):
    t, w2 = ug.shape
    w = w2 // 2
    groups = w // LANES
    ts = _tile(256, t)
    bias = jnp.repeat(b_s.T, LANES, axis=1)
    body = functools.partial(_sgu_body, groups=groups, nchunk=ts // CHUNK)
    return pl.pallas_call(
        body, out_shape=jax.ShapeDtypeStruct((t, w), BF16), grid=(t // ts,),
        in_specs=[pl.BlockSpec((ts, w), lambda i: (i, 0)),
                  pl.BlockSpec((ts, w), lambda i: (i, 1)),
                  pl.BlockSpec((1, w), lambda i: (0, 0)),
                  pl.BlockSpec((1, w), lambda i: (0, 0)),
                  pl.BlockSpec((groups, CHUNK, CHUNK), lambda i: (0, 0, 0)),
                  pl.BlockSpec((CHUNK, w), lambda i: (0, 0))],
        out_specs=pl.BlockSpec((ts, w), lambda i: (i, 0)),
        name="sgu", compiler_params=_cparams(("parallel",)),
    )(ug, ug, ln_g.reshape(1, w), ln_b.reshape(1, w), w_s, bias)


def _conv_body(x_ref, w_ref, b_ref, o_ref, prev_s):
    @pl.when(pl.program_id(2) == 0)
    def _():
        prev_s[...] = jnp.zeros(prev_s.shape, F32)

    x = x_ref[...]
    prev = prev_s[...]
    row = lax.broadcasted_iota(jnp.int32, x.shape, 0)
    acc = x * w_ref[CONV_K - 1:CONV_K, :] + b_ref[...]
    for k in range(1, CONV_K):
        xs = jnp.where(row < k, pltpu.roll(prev, k, 0), pltpu.roll(x, k, 0))
        acc = acc + xs * w_ref[CONV_K - 1 - k:CONV_K - k, :]
    o_ref[...] = acc * jax.nn.sigmoid(acc)
    prev_s[...] = x


def _conv_silu(xbc, w, b, bsz, seq):
    t, c = xbc.shape
    ts = _tile(512, seq)
    tc = _tile(2048, c)
    ns = seq // ts
    return pl.pallas_call(
        _conv_body, out_shape=jax.ShapeDtypeStruct((t, c), F32), grid=(c // tc, bsz, ns),
        in_specs=[pl.BlockSpec((ts, tc), lambda ci, bi, si: (bi * ns + si, ci)),
                  pl.BlockSpec((CONV_K, tc), lambda ci, bi, si: (0, ci)),
                  pl.BlockSpec((1, tc), lambda ci, bi, si: (0, ci))],
        out_specs=pl.BlockSpec((ts, tc), lambda ci, bi, si: (bi * ns + si, ci)),
        scratch_shapes=[pltpu.VMEM((ts, tc), F32)],
        name="conv_silu", compiler_params=_cparams(("parallel", "parallel", "arbitrary")),
    )(xbc, w, b.reshape(1, c))


def _softplus(x):
    return jnp.maximum(x, 0.0) + jnp.log1p(jnp.exp(-jnp.abs(x)))


def _ssd_body(x_ref, b_ref, c_ref, z_ref, dta_ref, dtb_ref, ba_ref, bb_ref, ala_ref, alb_ref,
              d_ref, ng_ref, o_ref, state_s, y_s, *, hpg):
    n = pl.program_id(2)
    L = CHUNK
    gw = hpg * SSD_HD

    @pl.when(n == 0)
    def _():
        state_s[...] = jnp.zeros(state_s.shape, F32)

    dt = _softplus(dta_ref[0] + ba_ref[0])
    dt_t = _softplus(dtb_ref[0] + bb_ref[0])
    da = dt * (-jnp.exp(ala_ref[0]))
    da_t = dt_t * (-jnp.exp(alb_ref[0]))
    ri = lax.broadcasted_iota(jnp.int32, (L, L), 0)
    ci = lax.broadcasted_iota(jnp.int32, (L, L), 1)
    causal = ci <= ri
    tri = causal.astype(BF16)
    tri_t = (ri <= ci).astype(BF16)
    acs = sum(_dot(tri, p) for p in _split3(da))
    acs_t = sum(_dot(p, tri_t) for p in _split3(da_t))
    acs_last = acs[L - 1:L, :]

    head_of_col = lax.broadcasted_iota(jnp.int32, (hpg, gw), 1) // SSD_HD
    expand = (head_of_col == lax.broadcasted_iota(jnp.int32, (hpg, gw), 0)).astype(BF16)

    def widen(v):
        return sum(_dot(p, expand) for p in _split2(v))

    dt_e = widen(dt)
    dec_e = widen(jnp.exp(acs))
    tail_e = widen(jnp.exp(acs_last - acs))

    x = x_ref[...]
    xdt = x * dt_e
    xdt_b = xdt.astype(BF16)
    bm = b_ref[...]
    cm_b = c_ref[...].astype(BF16)
    bm_b = bm.astype(BF16)
    cb = _dot_nt(cm_b, bm_b)
    state = state_s[...]
    y_s[...] = _dot(cm_b, state.astype(BF16)) * dec_e

    lane = lax.broadcasted_iota(jnp.int32, (L, LANES), 1)
    for pair in range(hpg // 2):
        cs = slice(pair * LANES, (pair + 1) * LANES)
        xp = xdt_b[:, cs]
        acc = None
        for half in range(2):
            r = 2 * pair + half
            seg = acs[:, r:r + 1] - acs_t[r:r + 1, :]
            decay = jnp.exp(jnp.where(causal, seg, NEG))
            mm = (cb * decay).astype(BF16)
            keep = (lane < SSD_HD) if half == 0 else (lane >= SSD_HD)
            part = _dot(mm, jnp.where(keep, xp, 0))
            acc = part if acc is None else acc + part
        y_s[:, cs] += acc

    state_s[...] = state * dec_e[L - 1:L, :] + _dot(bm.T.astype(BF16), (tail_e * xdt).astype(BF16))

    y = y_s[...] + x * d_ref[...]
    z = z_ref[...].astype(F32)
    y = y * (z * jax.nn.sigmoid(z))
    o_ref[...] = _rms(y, ng_ref[...]).astype(o_ref.dtype)


def _ssd_scan(xc, z, dt_raw, dt_bias, a_log, d_skip, norm_g, bsz, seq, d_inner):
    t = xc.shape[0]
    g, n_state = SSD_GROUPS, SSD_STATE
    heads = dt_raw.shape[1]
    hpg = heads // g
    gw = hpg * SSD_HD
    assert gw % LANES == 0 and hpg % 2 == 0
    nc = seq // CHUNK
    dta = dt_raw.reshape(t, g, hpg).transpose(1, 0, 2)
    dtb = dta.transpose(0, 2, 1)
    ba = dt_bias.reshape(g, 1, hpg)
    bb = dt_bias.reshape(g, hpg, 1)
    ala = a_log.reshape(g, 1, hpg)
    alb = a_log.reshape(g, hpg, 1)
    d_e = jnp.repeat(d_skip, SSD_HD).reshape(1, d_inner)
    xcol = gw // LANES
    b0 = d_inner // n_state
    c0 = (d_inner + g * n_state) // n_state
    row = lambda bi, ni: bi * nc + ni
    body = functools.partial(_ssd_body, hpg=hpg)
    return pl.pallas_call(
        body, out_shape=jax.ShapeDtypeStruct((t, d_inner), BF16), grid=(bsz, g, nc),
        in_specs=[pl.BlockSpec((CHUNK, gw), lambda bi, gi, ni: (row(bi, ni), gi)),
                  pl.BlockSpec((CHUNK, n_state), lambda bi, gi, ni: (row(bi, ni), b0 + gi)),
                  pl.BlockSpec((CHUNK, n_state), lambda bi, gi, ni: (row(bi, ni), c0 + gi)),
                  pl.BlockSpec((CHUNK, gw), lambda bi, gi, ni: (row(bi, ni), gi)),
                  pl.BlockSpec((1, CHUNK, hpg), lambda bi, gi, ni: (gi, row(bi, ni), 0)),
                  pl.BlockSpec((1, hpg, CHUNK), lambda bi, gi, ni: (gi, 0, row(bi, ni))),
                  pl.BlockSpec((1, 1, hpg), lambda bi, gi, ni: (gi, 0, 0)),
                  pl.BlockSpec((1, hpg, 1), lambda bi, gi, ni: (gi, 0, 0)),
                  pl.BlockSpec((1, 1, hpg), lambda bi, gi, ni: (gi, 0, 0)),
                  pl.BlockSpec((1, hpg, 1), lambda bi, gi, ni: (gi, 0, 0)),
                  pl.BlockSpec((1, gw), lambda bi, gi, ni: (0, gi)),
                  pl.BlockSpec((1, gw), lambda bi, gi, ni: (0, gi))],
        out_specs=pl.BlockSpec((CHUNK, gw), lambda bi, gi, ni: (row(bi, ni), gi)),
        scratch_shapes=[pltpu.VMEM((n_state, gw), F32), pltpu.VMEM((CHUNK, gw), F32)],
        name="ssd_scan", compiler_params=_cparams(("parallel", "parallel", "arbitrary")),
    )(xc, xc, xc, z, dta, dtb, ba, bb, ala, alb, d_e, norm_g.reshape(1, d_inner))


def _topk_rank(s, k_top, exact):
    n = s.shape[0]
    iota = lax.broadcasted_iota(jnp.int32, s.shape, 0).astype(F32)
    rank = jnp.full(s.shape, float(k_top), F32)
    vals = []
    for k in range(k_top):
        m = jnp.max(s, axis=0, keepdims=True)
        hit = s == m
        if exact:
            idx = jnp.min(jnp.where(hit, iota, float(n)), axis=0, keepdims=True)
            hit = iota == idx
        rank = jnp.where(hit, float(k), rank)
        s = jnp.where(hit, -jnp.inf, s)
        vals.append(m)
    return rank, jnp.concatenate(vals, axis=0)


_CANDS = [(a, b) for a in range(PEER_TOPK) for b in range(PEER_TOPK // (a + 1))]
N_CAND = 64


def _cand_tables():
    pa = np.zeros((N_CAND, PEER_TOPK), np.float32)
    pb = np.zeros((N_CAND, PEER_TOPK), np.float32)
    for r, (a, b) in enumerate(_CANDS):
        pa[r, a] = 1.0
        pb[r, b] = 1.0
    return jnp.asarray(pa, BF16), jnp.asarray(pb, BF16), jnp.asarray(pa.T, BF16)


def _pick_rows(onehot, v):
    hi, mid, lo = _split3(v)
    return (_dot(onehot, hi) + _dot(onehot, mid)) + _dot(onehot, lo)


def _route_compute(s1, s2, pa, pb, ga, exact):
    kt = PEER_TOPK
    r1, v1 = _topk_rank(s1, kt, exact)
    r2, v2 = _topk_rank(s2, kt, exact)
    c1 = _pick_rows(pa, v1)
    c2 = _pick_rows(pb, v2)
    row = lax.broadcasted_iota(jnp.int32, c1.shape, 0)
    cand = jnp.where(row < len(_CANDS), c1 + c2, -jnp.inf)
    crank, _ = _topk_rank(cand, kt, exact)
    sel = crank < float(kt)
    gates = jnp.where(sel, jnp.exp(c1 - v1[0:1]) * jnp.exp(c2 - v2[0:1]), 0.0)
    zsum = jnp.sum(gates, axis=0, keepdims=True)
    count = _dot(ga, sel.astype(F32).astype(BF16))
    lk = jnp.zeros(s1.shape, F32)
    for a in range(kt):
        lk = jnp.where(r1 == float(a), count[a:a + 1], lk)
    e1 = jnp.exp(s1 - v1[0:1])
    e2 = jnp.exp(s2 - v2[0:1]) / zsum * 0.5
    nsel = sum(jnp.sum((r < float(kt)).astype(F32), axis=0, keepdims=True) for r in (r1, r2, crank))
    return (lk, r2, e1, e2), nsel


def _route_body(q_ref, k_ref, pa_ref, pb_ref, ga_ref, lk_ref, r2_ref, e1_ref, e2_ref):
    q = q_ref[...]
    s1 = _dot_nt(k_ref[0, 0], q[:, :N_KEYS])
    s2 = _dot_nt(k_ref[0, 1], q[:, N_KEYS:])
    tabs = (pa_ref[...], pb_ref[...], ga_ref[...])

    def store(outs):
        for ref, val in zip((lk_ref, r2_ref, e1_ref, e2_ref), outs):
            ref[0] = val.astype(ref.dtype)

    outs, nsel = _route_compute(s1, s2, *tabs, exact=False)
    tied = jnp.max(jnp.abs(nsel - 3.0 * PEER_TOPK)) > 0.0

    @pl.when(jnp.logical_not(tied))
    def _():
        store(outs)

    @pl.when(tied)
    def _():
        store(_route_compute(s1, s2, *tabs, exact=True)[0])


def _peer_route(q, keys):
    t = q.shape[0]
    tt = _tile(512, t)
    f32_shp = jax.ShapeDtypeStruct((PEER_HEADS, N_KEYS, t), F32)
    bf16_shp = jax.ShapeDtypeStruct((PEER_HEADS, N_KEYS, t), BF16)
    ospec = pl.BlockSpec((1, N_KEYS, tt), lambda i, h: (h, 0, i))
    tab_spec = lambda shape: pl.BlockSpec(shape, lambda i, h: (0, 0))
    pa, pb, ga = _cand_tables()
    return pl.pallas_call(
        _route_body, out_shape=(f32_shp, bf16_shp, f32_shp, bf16_shp), grid=(t // tt, PEER_HEADS),
        in_specs=[pl.BlockSpec((tt, 2 * N_KEYS), lambda i, h: (i, h)),
                  pl.BlockSpec((1, 2, N_KEYS, N_KEYS), lambda i, h: (h, 0, 0, 0)),
                  tab_spec(pa.shape), tab_spec(pb.shape), tab_spec(ga.shape)],
        out_specs=(ospec, ospec, ospec, ospec),
        name="peer_route", compiler_params=_cparams(("parallel", "parallel")),
    )(q, keys, pa, pb, ga)


def _gelu_tanh(x, half_scale):
    c0 = math.sqrt(2.0 / math.pi)
    inner = x * (c0 + (c0 * 0.044715) * (x * x))
    return (x * half_scale) * (1.0 + jnp.tanh(inner))


def _peer_gate_epilogue(acc, o_ref, lk_ref, e1_ref, r2_ref, e2_ref):
    for ii in range(acc.shape[0] // N_KEYS):
        gate = None
        for h in range(PEER_HEADS):
            lk = lk_ref[h, ii:ii + 1, :].astype(BF16)
            e1 = e1_ref[h, ii:ii + 1, :].astype(BF16)
            term = e1 * jnp.where(r2_ref[h] < lk, e2_ref[h], jnp.zeros((), BF16))
            gate = term if gate is None else gate + term
        w = _gelu_tanh(acc[ii * N_KEYS:(ii + 1) * N_KEYS, :], gate.astype(F32))
        o_ref[ii * N_KEYS:(ii + 1) * N_KEYS, :] = w.astype(o_ref.dtype)


def kernel(x, c, ada_w1, ada_w2, ada_b, ada_table, norm_mix, norm_ffn, norm_final, hyb_w_in, hyb_w_out, diff_lam, diff_subln, sgu_ln_g, sgu_ln_b, sgu_w_s, sgu_b_s, ssd_w_in, ssd_conv_w, ssd_conv_b, ssd_dt_bias, ssd_a_log, ssd_d, ssd_norm, ssd_w_out, peer_w_q, peer_keys, peer_u, peer_v):
    bsz, seq, d = x.shape
    depth = ada_table.shape[0]
    t = bsz * seq
    n_experts = peer_u.shape[1]
    d_inner = ssd_norm.shape[1]
    ssd_heads = ssd_a_log.shape[1]
    conv_dim = ssd_conv_b.shape[1]
    sgu_w = sgu_ln_g.shape[1]
    att_w = hyb_w_out.shape[1] - sgu_w
    att_heads = att_w // (2 * ATT_HD)

    mod = _ada(c, ada_w1, ada_w2, ada_b, ada_table)
    xt = x.reshape(t, d)

    def gate_spec(l, col, tn):
        per = d // tn
        return lambda tm: (mod, (1, 1, tn), lambda i, j: (l * bsz + (i * tm) // seq, 0, col * per + j))

    for l in range(depth):
        h = _norm_mod(xt, norm_mix[l], mod, l * bsz, 0, 1, seq)
        if l % 2 == 0:
            e = l // 2
            lam_init = 0.8 - 0.6 * math.exp(-0.3 * l)
            qkv = _mm_w32(h, hyb_w_in, e, 0, 3 * att_w, tm=1024, tn=512,
                          out_shape=jax.ShapeDtypeStruct((t, 3 * att_w), BF16), name="hyb_qkv")
            ug = _mm_w32(h, hyb_w_in, e, 3 * att_w, 2 * sgu_w, tm=1024, tn=512,
                         out_shape=jax.ShapeDtypeStruct((t, 2 * sgu_w), F32), name="hyb_ug")
            a_out = _diff_attention(qkv, diff_lam[e], diff_subln[e], lam_init, bsz, seq, att_heads)
            s_out = _sgu(ug, sgu_ln_g[e], sgu_ln_b[e], sgu_w_s[e], sgu_b_s[e])
            y_in = jnp.concatenate([a_out, s_out], axis=-1)
            w_out = hyb_w_out[e].astype(BF16)
        else:
            o = l // 2
            z = _mm_w32(h, ssd_w_in, o, 0, d_inner, tm=1024, tn=512,
                        out_shape=jax.ShapeDtypeStruct((t, d_inner), F32), name="ssd_z")
            xbc = _mm_w32(h, ssd_w_in, o, d_inner, conv_dim, tm=1024, tn=512,
                          out_shape=jax.ShapeDtypeStruct((t, conv_dim), F32), name="ssd_xbc")
            dt_raw = _mm(h, ssd_w_in[o][:, d_inner + conv_dim:].astype(BF16), tm=1024, tn=ssd_heads, tk=d,
                         out_shape=jax.ShapeDtypeStruct((t, ssd_heads), F32), name="ssd_dt")
            xc = _conv_silu(xbc, ssd_conv_w[o], ssd_conv_b[o], bsz, seq)
            y_in = _ssd_scan(xc, z, dt_raw, ssd_dt_bias[o], ssd_a_log[o], ssd_d[o], ssd_norm[o],
                             bsz, seq, d_inner)
            w_out = ssd_w_out[o].astype(BF16)
        tm, tn = _tile(1024, seq), _tile(1024, d)
        xt = _mm(y_in, w_out, tm=tm, tn=tn, tk=2048,
                 out_shape=jax.ShapeDtypeStruct((t, d), F32),
                 extras=[(xt, (tm, tn), lambda i, j: (i, j)), gate_spec(l, 2, tn)(tm)],
                 epilogue=_resid_epilogue, name="mix_out")

        h = _norm_mod(xt, norm_ffn[l], mod, l * bsz, 3, 4, seq)
        q = _mm_w32(h, peer_w_q, l, 0, peer_w_q.shape[2], tm=1024, tn=512,
                    out_shape=jax.ShapeDtypeStruct((t, peer_w_q.shape[2]), BF16), name="peer_q")
        lk, r2, e1, e2 = _peer_route(q, peer_keys[l].astype(BF16))
        te, tt = _tile(1024, n_experts), _tile(512, t)
        ni1 = te // N_KEYS
        head_blk = (PEER_HEADS, ni1, tt)
        full_blk = (PEER_HEADS, N_KEYS, tt)
        wt = _mm(peer_u[l].astype(BF16), h, nt=True, tm=te, tn=tt, tk=d, n_outer=True,
                 out_shape=jax.ShapeDtypeStruct((n_experts, t), BF16),
                 extras=[(lk, head_blk, lambda i, j: (0, i, j)),
                         (e1, head_blk, lambda i, j: (0, i, j)),
                         (r2, full_blk, lambda i, j: (0, 0, j)),
                         (e2, full_blk, lambda i, j: (0, 0, j))],
                 epilogue=_peer_gate_epilogue, name="peer_gate")
        td, tt = _tile(1024, d), _tile(1024, seq)
        xt = _mm(peer_v[l].astype(BF16).T, wt, tm=td, tn=tt, tk=2048,
                 out_shape=jax.ShapeDtypeStruct((t, d), F32),
                 out_block=(tt, td), out_index=lambda i, j: (j, i),
                 extras=[(xt, (tt, td), lambda i, j: (j, i)),
                         (mod, (1, 1, td), lambda i, j, l=l, tt=tt, td=td:
                          (l * bsz + (j * tt) // seq, 0, 5 * (d // td) + i))],
                 epilogue=_resid_t_epilogue, name="peer_out")

    return _final_norm(xt, norm_final).reshape(bsz, seq, d)
```

```python
import functools
import math

import jax
import jax.numpy as jnp
import numpy as np
from jax import lax
from jax.experimental import pallas as pl
from jax.experimental.pallas import tpu as pltpu

F32 = jnp.float32
BF16 = jnp.bfloat16
NORM_EPS = 1e-6
LANES = 128
VMEM_LIMIT = 56 * 1024 * 1024
NEG = -1e30

ATT_HD = 64
SSD_HD = 64
SSD_GROUPS = 8
SSD_STATE = 128
CONV_K = 4
CHUNK = 128
PEER_HEADS = 8
N_KEYS = 128
PEER_TOPK = 16
ADA_N_MOD = 6


def _cparams(sem):
    return pltpu.CompilerParams(dimension_semantics=sem, vmem_limit_bytes=VMEM_LIMIT)


def _tile(pref, dim):
    t = min(pref, dim)
    while dim % t:
        t -= LANES
        assert t > 0, (pref, dim)
    return t


def _split2(a):
    hi = a.astype(BF16)
    lo = (a - hi.astype(F32)).astype(BF16)
    return hi, lo


def _split3(a):
    hi = a.astype(BF16)
    r = a - hi.astype(F32)
    mid = r.astype(BF16)
    lo = (r - mid.astype(F32)).astype(BF16)
    return hi, mid, lo


def _dot(a, b):
    return jnp.dot(a, b, preferred_element_type=F32)


def _dot_nt(a, b):
    return lax.dot_general(a, b, (((1,), (1,)), ((), ())), preferred_element_type=F32)


def _mm_body(*refs, nk, nt, n_extra, epilogue):
    a_ref, b_ref = refs[0], refs[1]
    extra = refs[2:2 + n_extra]
    o_ref = refs[2 + n_extra]
    dot = _dot_nt if nt else _dot
    if nk == 1:
        epilogue(dot(a_ref[...], b_ref[...]), o_ref, *extra)
        return
    acc_ref = refs[3 + n_extra]
    k = pl.program_id(2)

    @pl.when(k == 0)
    def _():
        acc_ref[...] = jnp.zeros(acc_ref.shape, F32)

    acc_ref[...] += dot(a_ref[...], b_ref[...])

    @pl.when(k == nk - 1)
    def _():
        epilogue(acc_ref[...], o_ref, *extra)


def _store_epilogue(acc, o_ref):
    o_ref[...] = acc.astype(o_ref.dtype)


def _mm(a, b, *, nt=False, tm, tn, tk, out_shape, out_block=None, out_index=None,
        extras=(), epilogue=_store_epilogue, n_outer=False, name="mm"):
    m, kdim = a.shape
    n = b.shape[0] if nt else b.shape[1]
    tm, tn, tk = _tile(tm, m), _tile(tn, n), _tile(tk, kdim)
    nm, nn, nk = m // tm, n // tn, kdim // tk
    if n_outer:
        grid = (nn, nm, nk)
        ij = lambda g0, g1: (g1, g0)
    else:
        grid = (nm, nn, nk)
        ij = lambda g0, g1: (g0, g1)

    def wrap(fn):
        return lambda g0, g1, k: fn(*ij(g0, g1), k)

    a_spec = pl.BlockSpec((tm, tk), wrap(lambda i, j, k: (i, k)))
    if nt:
        b_spec = pl.BlockSpec((tn, tk), wrap(lambda i, j, k: (j, k)))
    else:
        b_spec = pl.BlockSpec((tk, tn), wrap(lambda i, j, k: (k, j)))
    extra_specs = [pl.BlockSpec(blk, wrap(lambda i, j, k, f=f: f(i, j))) for _, blk, f in extras]
    if out_block is None:
        out_block, out_index = (tm, tn), (lambda i, j: (i, j))
    o_spec = pl.BlockSpec(out_block, wrap(lambda i, j, k: out_index(i, j)))
    scratch = [pltpu.VMEM((tm, tn), F32)] if nk > 1 else []
    body = functools.partial(_mm_body, nk=nk, nt=nt, n_extra=len(extras), epilogue=epilogue)
    return pl.pallas_call(
        body, out_shape=out_shape, grid=grid,
        in_specs=[a_spec, b_spec] + extra_specs, out_specs=o_spec,
        scratch_shapes=scratch, name=name,
        compiler_params=_cparams(("parallel", "parallel", "arbitrary")),
    )(a, b, *[e[0] for e in extras])


def _mm_w32_body(a_ref, b_ref, *refs, n_extra, epilogue):
    extra = refs[:n_extra]
    o_ref, bq_ref = refs[n_extra], refs[n_extra + 1]

    @pl.when(pl.program_id(1) == 0)
    def _():
        bq_ref[...] = b_ref[...].astype(BF16)

    epilogue(_dot(a_ref[...], bq_ref[...]), o_ref, *extra)


def _mm_w32(a, w, layer, col0, n, *, tm, tn, out_shape, extras=(), epilogue=_store_epilogue,
            name="mm_w32"):
    m, kdim = a.shape
    tm = _tile(tm, m)
    tn = min(tn, n)
    while n % tn or col0 % tn:
        tn -= LANES
        assert tn > 0, (n, col0)
    off = col0 // tn
    specs = [pl.BlockSpec((tm, kdim), lambda j, i: (i, 0)),
             pl.BlockSpec((None, kdim, tn), lambda j, i: (layer, 0, off + j))]
    specs += [pl.BlockSpec(blk, lambda j, i, f=f: f(i, j)) for _, blk, f in extras]
    body = functools.partial(_mm_w32_body, n_extra=len(extras), epilogue=epilogue)
    return pl.pallas_call(
        body, out_shape=out_shape, grid=(n // tn, m // tm),
        in_specs=specs, out_specs=pl.BlockSpec((tm, tn), lambda j, i: (i, j)),
        scratch_shapes=[pltpu.VMEM((kdim, tn), BF16)], name=name,
        compiler_params=_cparams(("parallel", "arbitrary")),
    )(a, w, *[e[0] for e in extras])


def _resid_epilogue(acc, o_ref, x_ref, g_ref):
    o_ref[...] = x_ref[...] + g_ref[0] * acc


def _resid_t_epilogue(acc, o_ref, x_ref, g_ref):
    o_ref[...] = x_ref[...] + g_ref[0] * acc.T


def _dot_split(a, b):
    a_hi, a_lo = _split2(a)
    b_hi, b_lo = _split2(b)
    return _dot(a_hi, b_hi) + _dot(a_hi, b_lo) + _dot(a_lo, b_hi)


def _ada_body(c_ref, w1_ref, w2_ref, b_ref, tab_ref, o_ref):
    c = c_ref[...]
    t = c * jax.nn.sigmoid(c)
    t1 = _dot_split(t, w1_ref[...])
    t0 = _dot_split(t1, w2_ref[...]) + b_ref[...]
    for l in range(o_ref.shape[0]):
        o_ref[l] = t0 + tab_ref[l:l + 1, :]


def _ada(c, w1, w2, b, table):
    bsz, d = c.shape
    depth, n = table.shape
    rank = w1.shape[1]
    bp = 8 * ((bsz + 7) // 8)
    cp = jnp.zeros((bp, d), F32).at[:bsz].set(c)
    tn = _tile(2048, d)
    out = pl.pallas_call(
        _ada_body, out_shape=jax.ShapeDtypeStruct((depth, bp, n), F32), grid=(n // tn,),
        in_specs=[pl.BlockSpec((bp, d), lambda j: (0, 0)),
                  pl.BlockSpec((d, rank), lambda j: (0, 0)),
                  pl.BlockSpec((rank, tn), lambda j: (0, j)),
                  pl.BlockSpec((1, tn), lambda j: (0, j)),
                  pl.BlockSpec((depth, tn), lambda j: (0, j))],
        out_specs=pl.BlockSpec((depth, bp, tn), lambda j: (0, 0, j)),
        name="ada", compiler_params=_cparams(("parallel",)),
    )(cp, w1, w2, b.reshape(1, n), table)
    return out[:, :bsz].reshape(depth * bsz, 1, n)


def _rms(x, g):
    return x * lax.rsqrt(jnp.mean(x * x, axis=-1, keepdims=True) + NORM_EPS) * g


def _norm_mod_body(x_ref, g_ref, sc_ref, sh_ref, o_ref):
    y = _rms(x_ref[...], g_ref[...])
    o_ref[...] = (y * (1.0 + sc_ref[0]) + sh_ref[0]).astype(o_ref.dtype)


def _norm_body(x_ref, g_ref, o_ref):
    o_ref[...] = _rms(x_ref[...], g_ref[...]).astype(o_ref.dtype)


def _norm_mod(x, g, mod, row0, sh_col, sc_col, seq):
    t, d = x.shape
    tm = _tile(512, seq)
    return pl.pallas_call(
        _norm_mod_body, out_shape=jax.ShapeDtypeStruct((t, d), BF16), grid=(t // tm,),
        in_specs=[pl.BlockSpec((tm, d), lambda i: (i, 0)),
                  pl.BlockSpec((1, d), lambda i: (0, 0)),
                  pl.BlockSpec((1, 1, d), lambda i: (row0 + (i * tm) // seq, 0, sc_col)),
                  pl.BlockSpec((1, 1, d), lambda i: (row0 + (i * tm) // seq, 0, sh_col))],
        out_specs=pl.BlockSpec((tm, d), lambda i: (i, 0)),
        name="norm_mod", compiler_params=_cparams(("parallel",)),
    )(x, g.reshape(1, d), mod, mod)


def _final_norm(x, g):
    t, d = x.shape
    tm = _tile(512, t)
    return pl.pallas_call(
        _norm_body, out_shape=jax.ShapeDtypeStruct((t, d), F32), grid=(t // tm,),
        in_specs=[pl.BlockSpec((tm, d), lambda i: (i, 0)),
                  pl.BlockSpec((1, d), lambda i: (0, 0))],
        out_specs=pl.BlockSpec((tm, d), lambda i: (i, 0)),
        name="final_norm", compiler_params=_cparams(("parallel",)),
    )(x, g.reshape(1, d))


def _attn_body(it_ref, jt_ref, q_ref, k_ref, v_ref, lam_ref, g_ref, o_ref,
               q_s, m_s, l_s, a_s, *, tq, tk, hb, lam_init):
    i = it_ref[pl.program_id(2)]
    j = jt_ref[pl.program_id(2)]
    vd = 2 * ATT_HD

    @pl.when(j == 0)
    def _():
        q = q_ref[...] * (ATT_HD ** -0.5)
        lane = lax.broadcasted_iota(jnp.int32, (tq, vd), 1)
        for hh in range(hb):
            qh = q[:, hh * vd:(hh + 1) * vd]
            q_s[hh, :tq, :] = jnp.where(lane < ATT_HD, qh, 0).astype(BF16)
            q_s[hh, tq:, :] = jnp.where(lane >= ATT_HD, qh, 0).astype(BF16)
        m_s[...] = jnp.full(m_s.shape, NEG, F32)
        l_s[...] = jnp.zeros(l_s.shape, F32)
        a_s[...] = jnp.zeros(a_s.shape, F32)

    def step(masked):
        if masked:
            krow = lax.broadcasted_iota(jnp.int32, (tk, 2 * tq), 0)
            qcol = lax.broadcasted_iota(jnp.int32, (tk, 2 * tq), 1)
            causal = krow <= jnp.where(qcol >= tq, qcol - tq, qcol)
        for hh in range(hb):
            k = k_ref[:, hh * vd:(hh + 1) * vd]
            v = v_ref[:, hh * vd:(hh + 1) * vd]
            s = _dot_nt(k, q_s[hh])
            if masked:
                s = jnp.where(causal, s, NEG)
            m_old = m_s[hh]
            m_new = jnp.maximum(m_old, jnp.max(s, axis=0, keepdims=True))
            alpha = jnp.exp(m_old - m_new)
            p = jnp.exp(s - m_new)
            l_s[hh] = alpha * l_s[hh] + jnp.sum(p, axis=0, keepdims=True)
            pv = lax.dot_general(v, p.astype(BF16), (((0,), (0,)), ((), ())),
                                 preferred_element_type=F32)
            a_s[hh] = alpha * a_s[hh] + pv
            m_s[hh] = m_new

    @pl.when(j < i)
    def _():
        step(False)

    @pl.when(j == i)
    def _():
        step(True)
        lp = lam_ref[...]
        lam = (jnp.exp(jnp.sum(lp[0:1] * lp[1:2], axis=-1, keepdims=True))
               - jnp.exp(jnp.sum(lp[2:3] * lp[3:4], axis=-1, keepdims=True)) + lam_init)
        for hh in range(hb):
            w = a_s[hh] / l_s[hh]
            o = w[:, :tq] - lam * w[:, tq:]
            o = o * lax.rsqrt(jnp.mean(o * o, axis=0, keepdims=True) + NORM_EPS) * g_ref[...]
            o_ref[:, hh * vd:(hh + 1) * vd] = (o * (1.0 - lam_init)).T.astype(o_ref.dtype)


def _diff_attention(qkv, lam_p, subln_g, lam_init, bsz, seq, heads):
    t = qkv.shape[0]
    vd = 2 * ATT_HD
    hb = 2 if heads % 2 == 0 else 1
    hblk = heads // hb
    tq = tk = _tile(512, seq)
    nq = seq // tq
    pairs = [(i, j) for i in range(nq) for j in range(i + 1)]
    i_tab = jnp.asarray([p[0] for p in pairs], jnp.int32)
    j_tab = jnp.asarray([p[1] for p in pairs], jnp.int32)
    body = functools.partial(_attn_body, tq=tq, tk=tk, hb=hb, lam_init=lam_init)
    grid_spec = pltpu.PrefetchScalarGridSpec(
        num_scalar_prefetch=2, grid=(bsz, hblk, len(pairs)),
        in_specs=[pl.BlockSpec((tq, hb * vd), lambda b, h, p, it, jt: (b * nq + it[p], h)),
                  pl.BlockSpec((tk, hb * vd), lambda b, h, p, it, jt: (b * nq + jt[p], hblk + h)),
                  pl.BlockSpec((tk, hb * vd), lambda b, h, p, it, jt: (b * nq + jt[p], 2 * hblk + h)),
                  pl.BlockSpec((4, ATT_HD), lambda b, h, p, it, jt: (0, 0)),
                  pl.BlockSpec((vd, 1), lambda b, h, p, it, jt: (0, 0))],
        out_specs=pl.BlockSpec((tq, hb * vd), lambda b, h, p, it, jt: (b * nq + it[p], h)),
        scratch_shapes=[pltpu.VMEM((hb, 2 * tq, vd), BF16), pltpu.VMEM((hb, 1, 2 * tq), F32),
                        pltpu.VMEM((hb, 1, 2 * tq), F32), pltpu.VMEM((hb, vd, 2 * tq), F32)])
    return pl.pallas_call(
        body, out_shape=jax.ShapeDtypeStruct((t, heads * vd), BF16), grid_spec=grid_spec,
        name="diff_attn",
        compiler_params=_cparams(("parallel", "parallel", "arbitrary")),
    )(i_tab, j_tab, qkv, qkv, qkv, lam_p, subln_g.reshape(vd, 1))


def _sgu_body(u_ref, v_ref, lng_ref, lnb_ref, w_ref, bias_ref, o_ref, *, groups, nchunk):
    u = jax.nn.gelu(u_ref[...])
    v = jax.nn.gelu(v_ref[...])
    mu = jnp.mean(v, axis=-1, keepdims=True)
    vc = v - mu
    var = jnp.mean(vc * vc, axis=-1, keepdims=True)
    vn = (vc * lax.rsqrt(var + NORM_EPS) * lng_ref[...] + lnb_ref[...]).astype(BF16)
    ri = lax.broadcasted_iota(jnp.int32, (CHUNK, CHUNK), 0)
    ci = lax.broadcasted_iota(jnp.int32, (CHUNK, CHUNK), 1)
    causal = ci <= ri
    for g in range(groups):
        cs = slice(g * LANES, (g + 1) * LANES)
        wg = jnp.where(causal, w_ref[g], 0.0).astype(BF16)
        bias = bias_ref[:, cs]
        for c in range(nchunk):
            rs = slice(c * CHUNK, (c + 1) * CHUNK)
            mixed = _dot(wg, vn[rs, cs]) + bias
            o_ref[rs, cs] = (u[rs, cs] * mixed).astype(o_ref.dtype)


def _sgu(ug, ln_g, ln_b, w_s, b_s):
    t, w2 = ug.shape
    w = w2 // 2
    groups = w // LANES
    ts = _tile(256, t)
    bias = jnp.repeat(b_s.T, LANES, axis=1)
    body = functools.partial(_sgu_body, groups=groups, nchunk=ts // CHUNK)
    return pl.pallas_call(
        body, out_shape=jax.ShapeDtypeStruct((t, w), BF16), grid=(t // ts,),
        in_specs=[pl.BlockSpec((ts, w), lambda i: (i, 0)),
                  pl.BlockSpec((ts, w), lambda i: (i, 1)),
                  pl.BlockSpec((1, w), lambda i: (0, 0)),
                  pl.BlockSpec((1, w), lambda i: (0, 0)),
                  pl.BlockSpec((groups, CHUNK, CHUNK), lambda i: (0, 0, 0)),
                  pl.BlockSpec((CHUNK, w), lambda i: (0, 0))],
        out_specs=pl.BlockSpec((ts, w), lambda i: (i, 0)),
        name="sgu", compiler_params=_cparams(("parallel",)),
    )(ug, ug, ln_g.reshape(1, w), ln_b.reshape(1, w), w_s, bias)


def _conv_body(x_ref, w_ref, b_ref, o_ref, prev_s):
    @pl.when(pl.program_id(2) == 0)
    def _():
        prev_s[...] = jnp.zeros(prev_s.shape, F32)

    x = x_ref[...]
    prev = prev_s[...]
    row = lax.broadcasted_iota(jnp.int32, x.shape, 0)
    acc = x * w_ref[CONV_K - 1:CONV_K, :] + b_ref[...]
    for k in range(1, CONV_K):
        xs = jnp.where(row < k, pltpu.roll(prev, k, 0), pltpu.roll(x, k, 0))
        acc = acc + xs * w_ref[CONV_K - 1 - k:CONV_K - k, :]
    o_ref[...] = acc * jax.nn.sigmoid(acc)
    prev_s[...] = x


def _conv_silu(xbc, w, b, bsz, seq):
    t, c = xbc.shape
    ts = _tile(512, seq)
    tc = _tile(2048, c)
    ns = seq // ts
    return pl.pallas_call(
        _conv_body, out_shape=jax.ShapeDtypeStruct((t, c), F32), grid=(c // tc, bsz, ns),
        in_specs=[pl.BlockSpec((ts, tc), lambda ci, bi, si: (bi * ns + si, ci)),
                  pl.BlockSpec((CONV_K, tc), lambda ci, bi, si: (0, ci)),
                  pl.BlockSpec((1, tc), lambda ci, bi, si: (0, ci))],
        out_specs=pl.BlockSpec((ts, tc), lambda ci, bi, si: (bi * ns + si, ci)),
        scratch_shapes=[pltpu.VMEM((ts, tc), F32)],
        name="conv_silu", compiler_params=_cparams(("parallel", "parallel", "arbitrary")),
    )(xbc, w, b.reshape(1, c))


def _softplus(x):
    return jnp.maximum(x, 0.0) + jnp.log1p(jnp.exp(-jnp.abs(x)))


def _ssd_body(x_ref, b_ref, c_ref, z_ref, dta_ref, dtb_ref, ba_ref, bb_ref, ala_ref, alb_ref,
              d_ref, ng_ref, o_ref, state_s, y_s, *, hpg):
    n = pl.program_id(2)
    L = CHUNK
    gw = hpg * SSD_HD

    @pl.when(n == 0)
    def _():
        state_s[...] = jnp.zeros(state_s.shape, F32)

    dt = _softplus(dta_ref[0] + ba_ref[0])
    dt_t = _softplus(dtb_ref[0] + bb_ref[0])
    da = dt * (-jnp.exp(ala_ref[0]))
    da_t = dt_t * (-jnp.exp(alb_ref[0]))
    ri = lax.broadcasted_iota(jnp.int32, (L, L), 0)
    ci = lax.broadcasted_iota(jnp.int32, (L, L), 1)
    causal = ci <= ri
    tri = causal.astype(BF16)
    tri_t = (ri <= ci).astype(BF16)
    acs = sum(_dot(tri, p) for p in _split3(da))
    acs_t = sum(_dot(p, tri_t) for p in _split3(da_t))
    acs_last = acs[L - 1:L, :]

    head_of_col = lax.broadcasted_iota(jnp.int32, (hpg, gw), 1) // SSD_HD
    expand = (head_of_col == lax.broadcasted_iota(jnp.int32, (hpg, gw), 0)).astype(BF16)

    def widen(v):
        return sum(_dot(p, expand) for p in _split2(v))

    dt_e = widen(dt)
    dec_e = widen(jnp.exp(acs))
    tail_e = widen(jnp.exp(acs_last - acs))

    x = x_ref[...]
    xdt = x * dt_e
    xdt_b = xdt.astype(BF16)
    bm = b_ref[...]
    cm_b = c_ref[...].astype(BF16)
    bm_b = bm.astype(BF16)
    cb = _dot_nt(cm_b, bm_b)
    state = state_s[...]
    y_s[...] = _dot(cm_b, state.astype(BF16)) * dec_e

    lane = lax.broadcasted_iota(jnp.int32, (L, LANES), 1)
    for pair in range(hpg // 2):
        cs = slice(pair * LANES, (pair + 1) * LANES)
        xp = xdt_b[:, cs]
        acc = None
        for half in range(2):
            r = 2 * pair + half
            seg = acs[:, r:r + 1] - acs_t[r:r + 1, :]
            decay = jnp.exp(jnp.where(causal, seg, NEG))
            mm = (cb * decay).astype(BF16)
            keep = (lane < SSD_HD) if half == 0 else (lane >= SSD_HD)
            part = _dot(mm, jnp.where(keep, xp, 0))
            acc = part if acc is None else acc + part
        y_s[:, cs] += acc

    state_s[...] = state * dec_e[L - 1:L, :] + _dot(bm.T.astype(BF16), (tail_e * xdt).astype(BF16))

    y = y_s[...] + x * d_ref[...]
    z = z_ref[...].astype(F32)
    y = y * (z * jax.nn.sigmoid(z))
    o_ref[...] = _rms(y, ng_ref[...]).astype(o_ref.dtype)


def _ssd_scan(xc, z, dt_raw, dt_bias, a_log, d_skip, norm_g, bsz, seq, d_inner):
    t = xc.shape[0]
    g, n_state = SSD_GROUPS, SSD_STATE
    heads = dt_raw.shape[1]
    hpg = heads // g
    gw = hpg * SSD_HD
    assert gw % LANES == 0 and hpg % 2 == 0
    nc = seq // CHUNK
    dta = dt_raw.reshape(t, g, hpg).transpose(1, 0, 2)
    dtb = dta.transpose(0, 2, 1)
    ba = dt_bias.reshape(g, 1, hpg)
    bb = dt_bias.reshape(g, hpg, 1)
    ala = a_log.reshape(g, 1, hpg)
    alb = a_log.reshape(g, hpg, 1)
    d_e = jnp.repeat(d_skip, SSD_HD).reshape(1, d_inner)
    xcol = gw // LANES
    b0 = d_inner // n_state
    c0 = (d_inner + g * n_state) // n_state
    row = lambda bi, ni: bi * nc + ni
    body = functools.partial(_ssd_body, hpg=hpg)
    return pl.pallas_call(
        body, out_shape=jax.ShapeDtypeStruct((t, d_inner), BF16), grid=(bsz, g, nc),
        in_specs=[pl.BlockSpec((CHUNK, gw), lambda bi, gi, ni: (row(bi, ni), gi)),
                  pl.BlockSpec((CHUNK, n_state), lambda bi, gi, ni: (row(bi, ni), b0 + gi)),
                  pl.BlockSpec((CHUNK, n_state), lambda bi, gi, ni: (row(bi, ni), c0 + gi)),
                  pl.BlockSpec((CHUNK, gw), lambda bi, gi, ni: (row(bi, ni), gi)),
                  pl.BlockSpec((1, CHUNK, hpg), lambda bi, gi, ni: (gi, row(bi, ni), 0)),
                  pl.BlockSpec((1, hpg, CHUNK), lambda bi, gi, ni: (gi, 0, row(bi, ni))),
                  pl.BlockSpec((1, 1, hpg), lambda bi, gi, ni: (gi, 0, 0)),
                  pl.BlockSpec((1, hpg, 1), lambda bi, gi, ni: (gi, 0, 0)),
                  pl.BlockSpec((1, 1, hpg), lambda bi, gi, ni: (gi, 0, 0)),
                  pl.BlockSpec((1, hpg, 1), lambda bi, gi, ni: (gi, 0, 0)),
                  pl.BlockSpec((1, gw), lambda bi, gi, ni: (0, gi)),
                  pl.BlockSpec((1, gw), lambda bi, gi, ni: (0, gi))],
        out_specs=pl.BlockSpec((CHUNK, gw), lambda bi, gi, ni: (row(bi, ni), gi)),
        scratch_shapes=[pltpu.VMEM((n_state, gw), F32), pltpu.VMEM((CHUNK, gw), F32)],
        name="ssd_scan", compiler_params=_cparams(("parallel", "parallel", "arbitrary")),
    )(xc, xc, xc, z, dta, dtb, ba, bb, ala, alb, d_e, norm_g.reshape(1, d_inner))


def _topk_rank(s, k_top, exact):
    n = s.shape[0]
    iota = lax.broadcasted_iota(jnp.int32, s.shape, 0).astype(F32)
    rank = jnp.full(s.shape, float(k_top), F32)
    vals = []
    for k in range(k_top):
        m = jnp.max(s, axis=0, keepdims=True)
        hit = s == m
        if exact:
            idx = jnp.min(jnp.where(hit, iota, float(n)), axis=0, keepdims=True)
            hit = iota == idx
        rank = jnp.where(hit, float(k), rank)
        s = jnp.where(hit, -jnp.inf, s)
        vals.append(m)
    return rank, jnp.concatenate(vals, axis=0)


_CANDS = [(a, b) for a in range(PEER_TOPK) for b in range(PEER_TOPK // (a + 1))]
N_CAND = 64


def _cand_tables():
    pa = np.zeros((N_CAND, PEER_TOPK), np.float32)
    pb = np.zeros((N_CAND, PEER_TOPK), np.float32)
    for r, (a, b) in enumerate(_CANDS):
        pa[r, a] = 1.0
        pb[r, b] = 1.0
    return jnp.asarray(pa, BF16), jnp.asarray(pb, BF16), jnp.asarray(pa.T, BF16)


def _pick_rows(onehot, v):
    hi, mid, lo = _split3(v)
    return (_dot(onehot, hi) + _dot(onehot, mid)) + _dot(onehot, lo)


def _route_compute(s1, s2, pa, pb, ga, exact):
    kt = PEER_TOPK
    r1, v1 = _topk_rank(s1, kt, exact)
    r2, v2 = _topk_rank(s2, kt, exact)
    c1 = _pick_rows(pa, v1)
    c2 = _pick_rows(pb, v2)
    row = lax.broadcasted_iota(jnp.int32, c1.shape, 0)
    cand = jnp.where(row < len(_CANDS), c1 + c2, -jnp.inf)
    crank, _ = _topk_rank(cand, kt, exact)
    sel = crank < float(kt)
    gates = jnp.where(sel, jnp.exp(c1 - v1[0:1]) * jnp.exp(c2 - v2[0:1]), 0.0)
    zsum = jnp.sum(gates, axis=0, keepdims=True)
    count = _dot(ga, sel.astype(F32).astype(BF16))
    lk = jnp.zeros(s1.shape, F32)
    for a in range(kt):
        lk = jnp.where(r1 == float(a), count[a:a + 1], lk)
    e1 = jnp.exp(s1 - v1[0:1])
    e2 = jnp.exp(s2 - v2[0:1]) / zsum * 0.5
    nsel = sum(jnp.sum((r < float(kt)).astype(F32), axis=0, keepdims=True) for r in (r1, r2, crank))
    return (lk, r2, e1, e2), nsel


def _route_body(q_ref, k_ref, pa_ref, pb_ref, ga_ref, lk_ref, r2_ref, e1_ref, e2_ref):
    q = q_ref[...]
    s1 = _dot_nt(k_ref[0, 0], q[:, :N_KEYS])
    s2 = _dot_nt(k_ref[0, 1], q[:, N_KEYS:])
    tabs = (pa_ref[...], pb_ref[...], ga_ref[...])

    def store(outs):
        for ref, val in zip((lk_ref, r2_ref, e1_ref, e2_ref), outs):
            ref[0] = val.astype(ref.dtype)

    outs, nsel = _route_compute(s1, s2, *tabs, exact=False)
    tied = jnp.max(jnp.abs(nsel - 3.0 * PEER_TOPK)) > 0.0

    @pl.when(jnp.logical_not(tied))
    def _():
        store(outs)

    @pl.when(tied)
    def _():
        store(_route_compute(s1, s2, *tabs, exact=True)[0])


def _peer_route(q, keys):
    t = q.shape[0]
    tt = _tile(512, t)
    f32_shp = jax.ShapeDtypeStruct((PEER_HEADS, N_KEYS, t), F32)
    bf16_shp = jax.ShapeDtypeStruct((PEER_HEADS, N_KEYS, t), BF16)
    ospec = pl.BlockSpec((1, N_KEYS, tt), lambda i, h: (h, 0, i))
    tab_spec = lambda shape: pl.BlockSpec(shape, lambda i, h: (0, 0))
    pa, pb, ga = _cand_tables()
    return pl.pallas_call(
        _route_body, out_shape=(f32_shp, bf16_shp, f32_shp, bf16_shp), grid=(t // tt, PEER_HEADS),
        in_specs=[pl.BlockSpec((tt, 2 * N_KEYS), lambda i, h: (i, h)),
                  pl.BlockSpec((1, 2, N_KEYS, N_KEYS), lambda i, h: (h, 0, 0, 0)),
                  tab_spec(pa.shape), tab_spec(pb.shape), tab_spec(ga.shape)],
        out_specs=(ospec, ospec, ospec, ospec),
        name="peer_route", compiler_params=_cparams(("parallel", "parallel")),
    )(q, keys, pa, pb, ga)


def _gelu_tanh(x, half_scale):
    c0 = math.sqrt(2.0 / math.pi)
    inner = x * (c0 + (c0 * 0.044715) * (x * x))
    return (x * half_scale) * (1.0 + jnp.tanh(inner))


def _peer_gate_epilogue(acc, o_ref, lk_ref, e1_ref, r2_ref, e2_ref):
    for ii in range(acc.shape[0] // N_KEYS):
        gate = None
        for h in range(PEER_HEADS):
            lk = lk_ref[h, ii:ii + 1, :].astype(BF16)
            e1 = e1_ref[h, ii:ii + 1, :].astype(BF16)
            term = e1 * jnp.where(r2_ref[h] < lk, e2_ref[h], jnp.zeros((), BF16))
            gate = term if gate is None else gate + term
        w = _gelu_tanh(acc[ii * N_KEYS:(ii + 1) * N_KEYS, :], gate.astype(F32))
        o_ref[ii * N_KEYS:(ii + 1) * N_KEYS, :] = w.astype(o_ref.dtype)


def kernel(x, c, ada_w1, ada_w2, ada_b, ada_table, norm_mix, norm_ffn, norm_final, hyb_w_in, hyb_w_out, diff_lam, diff_subln, sgu_ln_g, sgu_ln_b, sgu_w_s, sgu_b_s, ssd_w_in, ssd_conv_w, ssd_conv_b, ssd_dt_bias, ssd_a_log, ssd_d, ssd_norm, ssd_w_out, peer_w_q, peer_keys, peer_u, peer_v):
    bsz, seq, d = x.shape
    depth = ada_table.shape[0]
    t = bsz * seq
    n_experts = peer_u.shape[1]
    d_inner = ssd_norm.shape[1]
    ssd_heads = ssd_a_log.shape[1]
    conv_dim = ssd_conv_b.shape[1]
    sgu_w = sgu_ln_g.shape[1]
    att_w = hyb_w_out.shape[1] - sgu_w
    att_heads = att_w // (2 * ATT_HD)

    mod = _ada(c, ada_w1, ada_w2, ada_b, ada_table)
    xt = x.reshape(t, d)

    def gate_spec(l, col, tn):
        per = d // tn
        return lambda tm: (mod, (1, 1, tn), lambda i, j: (l * bsz + (i * tm) // seq, 0, col * per + j))

    for l in range(depth):
        h = _norm_mod(xt, norm_mix[l], mod, l * bsz, 0, 1, seq)
        if l % 2 == 0:
            e = l // 2
            lam_init = 0.8 - 0.6 * math.exp(-0.3 * l)
            w_in = hyb_w_in[e]
            qkv = _mm(h, w_in[:, :3 * att_w].astype(BF16), tm=1024, tn=1024, tk=d,
                      out_shape=jax.ShapeDtypeStruct((t, 3 * att_w), BF16), name="hyb_qkv")
            ug = _mm(h, w_in[:, 3 * att_w:].astype(BF16), tm=1024, tn=1024, tk=d,
                     out_shape=jax.ShapeDtypeStruct((t, 2 * sgu_w), F32), name="hyb_ug")
            a_out = _diff_attention(qkv, diff_lam[e], diff_subln[e], lam_init, bsz, seq, att_heads)
            s_out = _sgu(ug, sgu_ln_g[e], sgu_ln_b[e], sgu_w_s[e], sgu_b_s[e])
            y_in = jnp.concatenate([a_out, s_out], axis=-1)
            w_out = hyb_w_out[e].astype(BF16)
        else:
            o = l // 2
            w_in = ssd_w_in[o]
            z = _mm(h, w_in[:, :d_inner].astype(BF16), tm=1024, tn=1024, tk=d,
                    out_shape=jax.ShapeDtypeStruct((t, d_inner), F32), name="ssd_z")
            xbc = _mm(h, w_in[:, d_inner:d_inner + conv_dim].astype(BF16), tm=1024, tn=1024, tk=d,
                      out_shape=jax.ShapeDtypeStruct((t, conv_dim), F32), name="ssd_xbc")
            dt_shape = jax.ShapeDtypeStruct((t, ssd_heads), F32)
            if ssd_heads % LANES == 0:
                dt_raw = _mm_w32(h, ssd_w_in, o, d_inner + conv_dim, ssd_heads, tm=1024, tn=LANES,
                                 out_shape=dt_shape, name="ssd_dt")
            else:
                dt_raw = _mm(h, w_in[:, d_inner + conv_dim:].astype(BF16), tm=1024, tn=ssd_heads,
                             tk=d, out_shape=dt_shape, name="ssd_dt")
            xc = _conv_silu(xbc, ssd_conv_w[o], ssd_conv_b[o], bsz, seq)
            y_in = _ssd_scan(xc, z, dt_raw, ssd_dt_bias[o], ssd_a_log[o], ssd_d[o], ssd_norm[o],
                             bsz, seq, d_inner)
            w_out = ssd_w_out[o].astype(BF16)
        tm, tn = _tile(1024, seq), _tile(1024, d)
        xt = _mm(y_in, w_out, tm=tm, tn=tn, tk=2048,
                 out_shape=jax.ShapeDtypeStruct((t, d), F32),
                 extras=[(xt, (tm, tn), lambda i, j: (i, j)), gate_spec(l, 2, tn)(tm)],
                 epilogue=_resid_epilogue, name="mix_out")

        h = _norm_mod(xt, norm_ffn[l], mod, l * bsz, 3, 4, seq)
        q = _mm(h, peer_w_q[l].astype(BF16), tm=1024, tn=1024, tk=d,
                out_shape=jax.ShapeDtypeStruct((t, peer_w_q.shape[2]), BF16), name="peer_q")
        lk, r2, e1, e2 = _peer_route(q, peer_keys[l].astype(BF16))
        te, tt = _tile(1024, n_experts), _tile(512, t)
        ni1 = te // N_KEYS
        head_blk = (PEER_HEADS, ni1, tt)
        full_blk = (PEER_HEADS, N_KEYS, tt)
        wt = _mm(peer_u[l].astype(BF16), h, nt=True, tm=te, tn=tt, tk=d, n_outer=True,
                 out_shape=jax.ShapeDtypeStruct((n_experts, t), BF16),
                 extras=[(lk, head_blk, lambda i, j: (0, i, j)),
                         (e1, head_blk, lambda i, j: (0, i, j)),
                         (r2, full_blk, lambda i, j: (0, 0, j)),
                         (e2, full_blk, lambda i, j: (0, 0, j))],
                 epilogue=_peer_gate_epilogue, name="peer_gate")
        td, tt = _tile(1024, d), _tile(1024, seq)
        xt = _mm(peer_v[l].astype(BF16).T, wt, tm=td, tn=tt, tk=2048,
                 out_shape=jax.ShapeDtypeStruct((t, d), F32),
                 out_block=(tt, td), out_index=lambda i, j: (j, i),
                 extras=[(xt, (tt, td), lambda i, j: (j, i)),
                         (mod, (1, 1, td), lambda i, j, l=l, tt=tt, td=td:
                          (l * bsz + (j * tt) // seq, 0, 5 * (d // td) + i))],
                 epilogue=_resid_t_epilogue, name="peer_out")

    return _final_norm(xt, norm_final).reshape(bsz, seq, d)
```

```python
import functools
import math

import jax
import jax.numpy as jnp
import numpy as np
from jax import lax
from jax.experimental import pallas as pl
from jax.experimental.pallas import tpu as pltpu

F32 = jnp.float32
BF16 = jnp.bfloat16
NORM_EPS = 1e-6
LANES = 128
VMEM_LIMIT = 56 * 1024 * 1024
NEG = -1e30

ATT_HD = 64
SSD_HD = 64
SSD_GROUPS = 8
SSD_STATE = 128
CONV_K = 4
CHUNK = 128
PEER_HEADS = 8
N_KEYS = 128
PEER_TOPK = 16
ADA_N_MOD = 6


def _cparams(sem):
    return pltpu.CompilerParams(dimension_semantics=sem, vmem_limit_bytes=VMEM_LIMIT)


def _tile(pref, dim):
    t = min(pref, dim)
    while dim % t:
        t -= LANES
        assert t > 0, (pref, dim)
    return t


def _split2(a):
    hi = a.astype(BF16)
    lo = (a - hi.astype(F32)).astype(BF16)
    return hi, lo


def _split3(a):
    hi = a.astype(BF16)
    r = a - hi.astype(F32)
    mid = r.astype(BF16)
    lo = (r - mid.astype(F32)).astype(BF16)
    return hi, mid, lo


def _dot(a, b):
    return jnp.dot(a, b, preferred_element_type=F32)


def _dot_nt(a, b):
    return lax.dot_general(a, b, (((1,), (1,)), ((), ())), preferred_element_type=F32)


def _mm_body(*refs, nk, nt, n_extra, epilogue):
    a_ref, b_ref = refs[0], refs[1]
    extra = refs[2:2 + n_extra]
    o_ref = refs[2 + n_extra]
    dot = _dot_nt if nt else _dot
    if nk == 1:
        epilogue(dot(a_ref[...], b_ref[...]), o_ref, *extra)
        return
    acc_ref = refs[3 + n_extra]
    k = pl.program_id(2)

    @pl.when(k == 0)
    def _():
        acc_ref[...] = jnp.zeros(acc_ref.shape, F32)

    acc_ref[...] += dot(a_ref[...], b_ref[...])

    @pl.when(k == nk - 1)
    def _():
        epilogue(acc_ref[...], o_ref, *extra)


def _store_epilogue(acc, o_ref):
    o_ref[...] = acc.astype(o_ref.dtype)


def _mm(a, b, *, nt=False, tm, tn, tk, out_shape, out_block=None, out_index=None,
        extras=(), epilogue=_store_epilogue, n_outer=False, a_layer=None, b_layer=None,
        b_cols=None, name="mm"):
    m, kdim = a.shape[-2:]
    col0, n = b_cols if b_cols is not None else (0, b.shape[-2] if nt else b.shape[-1])
    tm, tn, tk = _tile(tm, m), _tile(tn, n), _tile(tk, kdim)
    while col0 % tn or n % tn:
        tn -= LANES
        assert tn > 0, (col0, n)
    coff = col0 // tn
    nm, nn, nk = m // tm, n // tn, kdim // tk
    if n_outer:
        grid = (nn, nm, nk)
        ij = lambda g0, g1: (g1, g0)
    else:
        grid = (nm, nn, nk)
        ij = lambda g0, g1: (g0, g1)

    def wrap(fn):
        return lambda g0, g1, k: fn(*ij(g0, g1), k)

    def spec(block, index, layer):
        if layer is None:
            return pl.BlockSpec(block, wrap(index))
        return pl.BlockSpec((None,) + block, wrap(lambda i, j, k: (layer,) + index(i, j, k)))

    a_spec = spec((tm, tk), lambda i, j, k: (i, k), a_layer)
    if nt:
        b_spec = spec((tn, tk), lambda i, j, k: (j, k), b_layer)
    else:
        b_spec = spec((tk, tn), lambda i, j, k: (k, coff + j), b_layer)
    extra_specs = [pl.BlockSpec(blk, wrap(lambda i, j, k, f=f: f(i, j))) for _, blk, f in extras]
    if out_block is None:
        out_block, out_index = (tm, tn), (lambda i, j: (i, j))
    o_spec = pl.BlockSpec(out_block, wrap(lambda i, j, k: out_index(i, j)))
    scratch = [pltpu.VMEM((tm, tn), F32)] if nk > 1 else []
    body = functools.partial(_mm_body, nk=nk, nt=nt, n_extra=len(extras), epilogue=epilogue)
    return pl.pallas_call(
        body, out_shape=out_shape, grid=grid,
        in_specs=[a_spec, b_spec] + extra_specs, out_specs=o_spec,
        scratch_shapes=scratch, name=name,
        compiler_params=_cparams(("parallel", "parallel", "arbitrary")),
    )(a, b, *[e[0] for e in extras])


def _resid_epilogue(acc, o_ref, x_ref, g_ref):
    o_ref[...] = x_ref[...] + g_ref[0] * acc


def _resid_t_epilogue(acc, o_ref, x_ref, g_ref):
    o_ref[...] = x_ref[...] + g_ref[0] * acc.T


def _dot_split(a, b):
    a_hi, a_lo = _split2(a)
    b_hi, b_lo = _split2(b)
    return _dot(a_hi, b_hi) + _dot(a_hi, b_lo) + _dot(a_lo, b_hi)


def _ada_body(c_ref, w1_ref, w2_ref, b_ref, tab_ref, o_ref):
    c = c_ref[...]
    t = c * jax.nn.sigmoid(c)
    t1 = _dot_split(t, w1_ref[...])
    t0 = _dot_split(t1, w2_ref[...]) + b_ref[...]
    for l in range(o_ref.shape[0]):
        o_ref[l] = t0 + tab_ref[l:l + 1, :]


def _ada(c, w1, w2, b, table):
    bsz, d = c.shape
    depth, n = table.shape
    rank = w1.shape[1]
    bp = 8 * ((bsz + 7) // 8)
    cp = jnp.zeros((bp, d), F32).at[:bsz].set(c)
    tn = _tile(2048, d)
    out = pl.pallas_call(
        _ada_body, out_shape=jax.ShapeDtypeStruct((depth, bp, n), F32), grid=(n // tn,),
        in_specs=[pl.BlockSpec((bp, d), lambda j: (0, 0)),
                  pl.BlockSpec((d, rank), lambda j: (0, 0)),
                  pl.BlockSpec((rank, tn), lambda j: (0, j)),
                  pl.BlockSpec((1, tn), lambda j: (0, j)),
                  pl.BlockSpec((depth, tn), lambda j: (0, j))],
        out_specs=pl.BlockSpec((depth, bp, tn), lambda j: (0, 0, j)),
        name="ada", compiler_params=_cparams(("parallel",)),
    )(cp, w1, w2, b.reshape(1, n), table)
    return out[:, :bsz].reshape(depth * bsz, 1, n)


def _rms(x, g):
    return x * lax.rsqrt(jnp.mean(x * x, axis=-1, keepdims=True) + NORM_EPS) * g


def _norm_mod_body(x_ref, g_ref, sc_ref, sh_ref, o_ref):
    y = _rms(x_ref[...], g_ref[...])
    o_ref[...] = (y * (1.0 + sc_ref[0]) + sh_ref[0]).astype(o_ref.dtype)


def _norm_body(x_ref, g_ref, o_ref):
    o_ref[...] = _rms(x_ref[...], g_ref[...]).astype(o_ref.dtype)


def _norm_mod(x, g, mod, row0, sh_col, sc_col, seq):
    t, d = x.shape
    tm = _tile(512, seq)
    return pl.pallas_call(
        _norm_mod_body, out_shape=jax.ShapeDtypeStruct((t, d), BF16), grid=(t // tm,),
        in_specs=[pl.BlockSpec((tm, d), lambda i: (i, 0)),
                  pl.BlockSpec((1, d), lambda i: (0, 0)),
                  pl.BlockSpec((1, 1, d), lambda i: (row0 + (i * tm) // seq, 0, sc_col)),
                  pl.BlockSpec((1, 1, d), lambda i: (row0 + (i * tm) // seq, 0, sh_col))],
        out_specs=pl.BlockSpec((tm, d), lambda i: (i, 0)),
        name="norm_mod", compiler_params=_cparams(("parallel",)),
    )(x, g.reshape(1, d), mod, mod)


def _final_norm(x, g):
    t, d = x.shape
    tm = _tile(512, t)
    return pl.pallas_call(
        _norm_body, out_shape=jax.ShapeDtypeStruct((t, d), F32), grid=(t // tm,),
        in_specs=[pl.BlockSpec((tm, d), lambda i: (i, 0)),
                  pl.BlockSpec((1, d), lambda i: (0, 0))],
        out_specs=pl.BlockSpec((tm, d), lambda i: (i, 0)),
        name="final_norm", compiler_params=_cparams(("parallel",)),
    )(x, g.reshape(1, d))


def _attn_body(it_ref, jt_ref, q_ref, k_ref, v_ref, lam_ref, g_ref, o_ref,
               q_s, m_s, l_s, a_s, *, tq, tk, hb, lam_init):
    i = it_ref[pl.program_id(2)]
    j = jt_ref[pl.program_id(2)]
    vd = 2 * ATT_HD

    @pl.when(j == 0)
    def _():
        q = q_ref[...] * (ATT_HD ** -0.5)
        lane = lax.broadcasted_iota(jnp.int32, (tq, vd), 1)
        for hh in range(hb):
            qh = q[:, hh * vd:(hh + 1) * vd]
            q_s[hh, :tq, :] = jnp.where(lane < ATT_HD, qh, 0).astype(BF16)
            q_s[hh, tq:, :] = jnp.where(lane >= ATT_HD, qh, 0).astype(BF16)
        m_s[...] = jnp.full(m_s.shape, NEG, F32)
        l_s[...] = jnp.zeros(l_s.shape, F32)
        a_s[...] = jnp.zeros(a_s.shape, F32)

    def step(masked):
        if masked:
            krow = lax.broadcasted_iota(jnp.int32, (tk, 2 * tq), 0)
            qcol = lax.broadcasted_iota(jnp.int32, (tk, 2 * tq), 1)
            causal = krow <= jnp.where(qcol >= tq, qcol - tq, qcol)
        for hh in range(hb):
            k = k_ref[:, hh * vd:(hh + 1) * vd]
            v = v_ref[:, hh * vd:(hh + 1) * vd]
            s = _dot_nt(k, q_s[hh])
            if masked:
                s = jnp.where(causal, s, NEG)
            m_old = m_s[hh]
            m_new = jnp.maximum(m_old, jnp.max(s, axis=0, keepdims=True))
            alpha = jnp.exp(m_old - m_new)
            p = jnp.exp(s - m_new)
            l_s[hh] = alpha * l_s[hh] + jnp.sum(p, axis=0, keepdims=True)
            pv = lax.dot_general(v, p.astype(BF16), (((0,), (0,)), ((), ())),
                                 preferred_element_type=F32)
            a_s[hh] = alpha * a_s[hh] + pv
            m_s[hh] = m_new

    @pl.when(j < i)
    def _():
        step(False)

    @pl.when(j == i)
    def _():
        step(True)
        lp = lam_ref[...]
        lam = (jnp.exp(jnp.sum(lp[0:1] * lp[1:2], axis=-1, keepdims=True))
               - jnp.exp(jnp.sum(lp[2:3] * lp[3:4], axis=-1, keepdims=True)) + lam_init)
        for hh in range(hb):
            w = a_s[hh] / l_s[hh]
            o = w[:, :tq] - lam * w[:, tq:]
            o = o * lax.rsqrt(jnp.mean(o * o, axis=0, keepdims=True) + NORM_EPS) * g_ref[...]
            o_ref[:, hh * vd:(hh + 1) * vd] = (o * (1.0 - lam_init)).T.astype(o_ref.dtype)


def _diff_attention(qkv, lam_p, subln_g, lam_init, bsz, seq, heads):
    t = qkv.shape[0]
    vd = 2 * ATT_HD
    hb = 2 if heads % 2 == 0 else 1
    hblk = heads // hb
    tq = tk = _tile(512, seq)
    nq = seq // tq
    pairs = [(i, j) for i in range(nq) for j in range(i + 1)]
    i_tab = jnp.asarray([p[0] for p in pairs], jnp.int32)
    j_tab = jnp.asarray([p[1] for p in pairs], jnp.int32)
    body = functools.partial(_attn_body, tq=tq, tk=tk, hb=hb, lam_init=lam_init)
    grid_spec = pltpu.PrefetchScalarGridSpec(
        num_scalar_prefetch=2, grid=(bsz, hblk, len(pairs)),
        in_specs=[pl.BlockSpec((tq, hb * vd), lambda b, h, p, it, jt: (b * nq + it[p], h)),
                  pl.BlockSpec((tk, hb * vd), lambda b, h, p, it, jt: (b * nq + jt[p], hblk + h)),
                  pl.BlockSpec((tk, hb * vd), lambda b, h, p, it, jt: (b * nq + jt[p], 2 * hblk + h)),
                  pl.BlockSpec((4, ATT_HD), lambda b, h, p, it, jt: (0, 0)),
                  pl.BlockSpec((vd, 1), lambda b, h, p, it, jt: (0, 0))],
        out_specs=pl.BlockSpec((tq, hb * vd), lambda b, h, p, it, jt: (b * nq + it[p], h)),
        scratch_shapes=[pltpu.VMEM((hb, 2 * tq, vd), BF16), pltpu.VMEM((hb, 1, 2 * tq), F32),
                        pltpu.VMEM((hb, 1, 2 * tq), F32), pltpu.VMEM((hb, vd, 2 * tq), F32)])
    return pl.pallas_call(
        body, out_shape=jax.ShapeDtypeStruct((t, heads * vd), BF16), grid_spec=grid_spec,
        name="diff_attn",
        compiler_params=_cparams(("parallel", "parallel", "arbitrary")),
    )(i_tab, j_tab, qkv, qkv, qkv, lam_p, subln_g.reshape(vd, 1))


def _sgu_body(u_ref, v_ref, lng_ref, lnb_ref, w_ref, bias_ref, o_ref, *, groups, nchunk):
    u = jax.nn.gelu(u_ref[...])
    v = jax.nn.gelu(v_ref[...])
    mu = jnp.mean(v, axis=-1, keepdims=True)
    vc = v - mu
    var = jnp.mean(vc * vc, axis=-1, keepdims=True)
    vn = (vc * lax.rsqrt(var + NORM_EPS) * lng_ref[...] + lnb_ref[...]).astype(BF16)
    ri = lax.broadcasted_iota(jnp.int32, (CHUNK, CHUNK), 0)
    ci = lax.broadcasted_iota(jnp.int32, (CHUNK, CHUNK), 1)
    causal = ci <= ri
    for g in range(groups):
        cs = slice(g * LANES, (g + 1) * LANES)
        wg = jnp.where(causal, w_ref[g], 0.0).astype(BF16)
        bias = bias_ref[:, cs]
        for c in range(nchunk):
            rs = slice(c * CHUNK, (c + 1) * CHUNK)
            mixed = _dot(wg, vn[rs, cs]) + bias
            o_ref[rs, cs] = (u[rs, cs] * mixed).astype(o_ref.dtype)


def _sgu(ug, ln_g, ln_b, w_s, b_s):
    t, w2 = ug.shape
    w = w2 // 2
    groups = w // LANES
    ts = _tile(256, t)
    bias = jnp.repeat(b_s.T, LANES, axis=1)
    body = functools.partial(_sgu_body, groups=groups, nchunk=ts // CHUNK)
    return pl.pallas_call(
        body, out_shape=jax.ShapeDtypeStruct((t, w), BF16), grid=(t // ts,),
        in_specs=[pl.BlockSpec((ts, w), lambda i: (i, 0)),
                  pl.BlockSpec((ts, w), lambda i: (i, 1)),
                  pl.BlockSpec((1, w), lambda i: (0, 0)),
                  pl.BlockSpec((1, w), lambda i: (0, 0)),
                  pl.BlockSpec((groups, CHUNK, CHUNK), lambda i: (0, 0, 0)),
                  pl.BlockSpec((CHUNK, w), lambda i: (0, 0))],
        out_specs=pl.BlockSpec((ts, w), lambda i: (i, 0)),
        name="sgu", compiler_params=_cparams(("parallel",)),
    )(ug, ug, ln_g.reshape(1, w), ln_b.reshape(1, w), w_s, bias)


def _conv_body(x_ref, w_ref, b_ref, o_ref, prev_s):
    @pl.when(pl.program_id(2) == 0)
    def _():
        prev_s[...] = jnp.zeros(prev_s.shape, F32)

    x = x_ref[...]
    prev = prev_s[...]
    row = lax.broadcasted_iota(jnp.int32, x.shape, 0)
    acc = x * w_ref[CONV_K - 1:CONV_K, :] + b_ref[...]
    for k in range(1, CONV_K):
        xs = jnp.where(row < k, pltpu.roll(prev, k, 0), pltpu.roll(x, k, 0))
        acc = acc + xs * w_ref[CONV_K - 1 - k:CONV_K - k, :]
    o_ref[...] = acc * jax.nn.sigmoid(acc)
    prev_s[...] = x


def _conv_silu(xbc, w, b, bsz, seq):
    t, c = xbc.shape
    ts = _tile(512, seq)
    tc = _tile(2048, c)
    ns = seq // ts
    return pl.pallas_call(
        _conv_body, out_shape=jax.ShapeDtypeStruct((t, c), F32), grid=(c // tc, bsz, ns),
        in_specs=[pl.BlockSpec((ts, tc), lambda ci, bi, si: (bi * ns + si, ci)),
                  pl.BlockSpec((CONV_K, tc), lambda ci, bi, si: (0, ci)),
                  pl.BlockSpec((1, tc), lambda ci, bi, si: (0, ci))],
        out_specs=pl.BlockSpec((ts, tc), lambda ci, bi, si: (bi * ns + si, ci)),
        scratch_shapes=[pltpu.VMEM((ts, tc), F32)],
        name="conv_silu", compiler_params=_cparams(("parallel", "parallel", "arbitrary")),
    )(xbc, w, b.reshape(1, c))


def _softplus(x):
    return jnp.maximum(x, 0.0) + jnp.log1p(jnp.exp(-jnp.abs(x)))


def _ssd_body(x_ref, b_ref, c_ref, z_ref, dta_ref, dtb_ref, ba_ref, bb_ref, ala_ref, alb_ref,
              d_ref, ng_ref, o_ref, state_s, y_s, *, hpg):
    n = pl.program_id(2)
    L = CHUNK
    gw = hpg * SSD_HD

    @pl.when(n == 0)
    def _():
        state_s[...] = jnp.zeros(state_s.shape, F32)

    dt = _softplus(dta_ref[0] + ba_ref[0])
    dt_t = _softplus(dtb_ref[0] + bb_ref[0])
    da = dt * (-jnp.exp(ala_ref[0]))
    da_t = dt_t * (-jnp.exp(alb_ref[0]))
    ri = lax.broadcasted_iota(jnp.int32, (L, L), 0)
    ci = lax.broadcasted_iota(jnp.int32, (L, L), 1)
    causal = ci <= ri
    tri = causal.astype(BF16)
    tri_t = (ri <= ci).astype(BF16)
    acs = sum(_dot(tri, p) for p in _split3(da))
    acs_t = sum(_dot(p, tri_t) for p in _split3(da_t))
    acs_last = acs[L - 1:L, :]

    head_of_col = lax.broadcasted_iota(jnp.int32, (hpg, gw), 1) // SSD_HD
    expand = (head_of_col == lax.broadcasted_iota(jnp.int32, (hpg, gw), 0)).astype(BF16)

    def widen(v):
        return sum(_dot(p, expand) for p in _split2(v))

    dt_e = widen(dt)
    dec_e = widen(jnp.exp(acs))
    tail_e = widen(jnp.exp(acs_last - acs))

    x = x_ref[...]
    xdt = x * dt_e
    xdt_b = xdt.astype(BF16)
    bm = b_ref[...]
    cm_b = c_ref[...].astype(BF16)
    bm_b = bm.astype(BF16)
    cb = _dot_nt(cm_b, bm_b)
    state = state_s[...]
    y_s[...] = _dot(cm_b, state.astype(BF16)) * dec_e

    lane = lax.broadcasted_iota(jnp.int32, (L, LANES), 1)
    for pair in range(hpg // 2):
        cs = slice(pair * LANES, (pair + 1) * LANES)
        xp = xdt_b[:, cs]
        acc = None
        for half in range(2):
            r = 2 * pair + half
            seg = acs[:, r:r + 1] - acs_t[r:r + 1, :]
            decay = jnp.exp(jnp.where(causal, seg, NEG))
            mm = (cb * decay).astype(BF16)
            keep = (lane < SSD_HD) if half == 0 else (lane >= SSD_HD)
            part = _dot(mm, jnp.where(keep, xp, 0))
            acc = part if acc is None else acc + part
        y_s[:, cs] += acc

    state_s[...] = state * dec_e[L - 1:L, :] + _dot(bm.T.astype(BF16), (tail_e * xdt).astype(BF16))

    y = y_s[...] + x * d_ref[...]
    z = z_ref[...].astype(F32)
    y = y * (z * jax.nn.sigmoid(z))
    o_ref[...] = _rms(y, ng_ref[...]).astype(o_ref.dtype)


def _ssd_scan(xc, z, dt_raw, dt_bias, a_log, d_skip, norm_g, bsz, seq, d_inner):
    t = xc.shape[0]
    g, n_state = SSD_GROUPS, SSD_STATE
    heads = dt_raw.shape[1]
    hpg = heads // g
    gw = hpg * SSD_HD
    assert gw % LANES == 0 and hpg % 2 == 0
    nc = seq // CHUNK
    dta = dt_raw.reshape(t, g, hpg).transpose(1, 0, 2)
    dtb = dta.transpose(0, 2, 1)
    ba = dt_bias.reshape(g, 1, hpg)
    bb = dt_bias.reshape(g, hpg, 1)
    ala = a_log.reshape(g, 1, hpg)
    alb = a_log.reshape(g, hpg, 1)
    d_e = jnp.repeat(d_skip, SSD_HD).reshape(1, d_inner)
    xcol = gw // LANES
    b0 = d_inner // n_state
    c0 = (d_inner + g * n_state) // n_state
    row = lambda bi, ni: bi * nc + ni
    body = functools.partial(_ssd_body, hpg=hpg)
    return pl.pallas_call(
        body, out_shape=jax.ShapeDtypeStruct((t, d_inner), BF16), grid=(bsz, g, nc),
        in_specs=[pl.BlockSpec((CHUNK, gw), lambda bi, gi, ni: (row(bi, ni), gi)),
                  pl.BlockSpec((CHUNK, n_state), lambda bi, gi, ni: (row(bi, ni), b0 + gi)),
                  pl.BlockSpec((CHUNK, n_state), lambda bi, gi, ni: (row(bi, ni), c0 + gi)),
                  pl.BlockSpec((CHUNK, gw), lambda bi, gi, ni: (row(bi, ni), gi)),
                  pl.BlockSpec((1, CHUNK, hpg), lambda bi, gi, ni: (gi, row(bi, ni), 0)),
                  pl.BlockSpec((1, hpg, CHUNK), lambda bi, gi, ni: (gi, 0, row(bi, ni))),
                  pl.BlockSpec((1, 1, hpg), lambda bi, gi, ni: (gi, 0, 0)),
                  pl.BlockSpec((1, hpg, 1), lambda bi, gi, ni: (gi, 0, 0)),
                  pl.BlockSpec((1, 1, hpg), lambda bi, gi, ni: (gi, 0, 0)),
                  pl.BlockSpec((1, hpg, 1), lambda bi, gi, ni: (gi, 0, 0)),
                  pl.BlockSpec((1, gw), lambda bi, gi, ni: (0, gi)),
                  pl.BlockSpec((1, gw), lambda bi, gi, ni: (0, gi))],
        out_specs=pl.BlockSpec((CHUNK, gw), lambda bi, gi, ni: (row(bi, ni), gi)),
        scratch_shapes=[pltpu.VMEM((n_state, gw), F32), pltpu.VMEM((CHUNK, gw), F32)],
        name="ssd_scan", compiler_params=_cparams(("parallel", "parallel", "arbitrary")),
    )(xc, xc, xc, z, dta, dtb, ba, bb, ala, alb, d_e, norm_g.reshape(1, d_inner))


def _topk_rank(s, k_top, exact):
    n = s.shape[0]
    iota = lax.broadcasted_iota(jnp.int32, s.shape, 0).astype(F32)
    rank = jnp.full(s.shape, float(k_top), F32)
    vals = []
    for k in range(k_top):
        m = jnp.max(s, axis=0, keepdims=True)
        hit = s == m
        if exact:
            idx = jnp.min(jnp.where(hit, iota, float(n)), axis=0, keepdims=True)
            hit = iota == idx
        rank = jnp.where(hit, float(k), rank)
        s = jnp.where(hit, -jnp.inf, s)
        vals.append(m)
    return rank, jnp.concatenate(vals, axis=0)


_CANDS = [(a, b) for a in range(PEER_TOPK) for b in range(PEER_TOPK // (a + 1))]
N_CAND = 64


def _cand_tables():
    pa = np.zeros((N_CAND, PEER_TOPK), np.float32)
    pb = np.zeros((N_CAND, PEER_TOPK), np.float32)
    for r, (a, b) in enumerate(_CANDS):
        pa[r, a] = 1.0
        pb[r, b] = 1.0
    return jnp.asarray(pa, BF16), jnp.asarray(pb, BF16), jnp.asarray(pa.T, BF16)


def _pick_rows(onehot, v):
    hi, mid, lo = _split3(v)
    return (_dot(onehot, hi) + _dot(onehot, mid)) + _dot(onehot, lo)


def _route_compute(s1, s2, pa, pb, ga, exact):
    kt = PEER_TOPK
    r1, v1 = _topk_rank(s1, kt, exact)
    r2, v2 = _topk_rank(s2, kt, exact)
    c1 = _pick_rows(pa, v1)
    c2 = _pick_rows(pb, v2)
    row = lax.broadcasted_iota(jnp.int32, c1.shape, 0)
    cand = jnp.where(row < len(_CANDS), c1 + c2, -jnp.inf)
    crank, _ = _topk_rank(cand, kt, exact)
    sel = crank < float(kt)
    gates = jnp.where(sel, jnp.exp(c1 - v1[0:1]) * jnp.exp(c2 - v2[0:1]), 0.0)
    zsum = jnp.sum(gates, axis=0, keepdims=True)
    count = _dot(ga, sel.astype(F32).astype(BF16))
    lk = jnp.zeros(s1.shape, F32)
    for a in range(kt):
        lk = jnp.where(r1 == float(a), count[a:a + 1], lk)
    e1 = jnp.exp(s1 - v1[0:1])
    e2 = jnp.exp(s2 - v2[0:1]) / zsum * 0.5
    nsel = sum(jnp.sum((r < float(kt)).astype(F32), axis=0, keepdims=True) for r in (r1, r2, crank))
    return (lk, r2, e1, e2), nsel


def _route_body(q_ref, k_ref, pa_ref, pb_ref, ga_ref, lk_ref, r2_ref, e1_ref, e2_ref):
    q = q_ref[...]
    s1 = _dot_nt(k_ref[0, 0], q[:, :N_KEYS])
    s2 = _dot_nt(k_ref[0, 1], q[:, N_KEYS:])
    tabs = (pa_ref[...], pb_ref[...], ga_ref[...])

    def store(outs):
        for ref, val in zip((lk_ref, r2_ref, e1_ref, e2_ref), outs):
            ref[0] = val.astype(ref.dtype)

    outs, nsel = _route_compute(s1, s2, *tabs, exact=False)
    tied = jnp.max(jnp.abs(nsel - 3.0 * PEER_TOPK)) > 0.0

    @pl.when(jnp.logical_not(tied))
    def _():
        store(outs)

    @pl.when(tied)
    def _():
        store(_route_compute(s1, s2, *tabs, exact=True)[0])


def _peer_route(q, keys):
    t = q.shape[0]
    tt = _tile(512, t)
    f32_shp = jax.ShapeDtypeStruct((PEER_HEADS, N_KEYS, t), F32)
    bf16_shp = jax.ShapeDtypeStruct((PEER_HEADS, N_KEYS, t), BF16)
    ospec = pl.BlockSpec((1, N_KEYS, tt), lambda i, h: (h, 0, i))
    tab_spec = lambda shape: pl.BlockSpec(shape, lambda i, h: (0, 0))
    pa, pb, ga = _cand_tables()
    return pl.pallas_call(
        _route_body, out_shape=(f32_shp, bf16_shp, f32_shp, bf16_shp), grid=(t // tt, PEER_HEADS),
        in_specs=[pl.BlockSpec((tt, 2 * N_KEYS), lambda i, h: (i, h)),
                  pl.BlockSpec((1, 2, N_KEYS, N_KEYS), lambda i, h: (h, 0, 0, 0)),
                  tab_spec(pa.shape), tab_spec(pb.shape), tab_spec(ga.shape)],
        out_specs=(ospec, ospec, ospec, ospec),
        name="peer_route", compiler_params=_cparams(("parallel", "parallel")),
    )(q, keys, pa, pb, ga)


def _gelu_tanh(x, half_scale):
    c0 = math.sqrt(2.0 / math.pi)
    inner = x * (c0 + (c0 * 0.044715) * (x * x))
    return (x * half_scale) * (1.0 + jnp.tanh(inner))


def _peer_gate_epilogue(acc, o_ref, lk_ref, e1_ref, r2_ref, e2_ref):
    for ii in range(acc.shape[0] // N_KEYS):
        gate = None
        for h in range(PEER_HEADS):
            lk = lk_ref[h, ii:ii + 1, :].astype(BF16)
            e1 = e1_ref[h, ii:ii + 1, :].astype(BF16)
            term = e1 * jnp.where(r2_ref[h] < lk, e2_ref[h], jnp.zeros((), BF16))
            gate = term if gate is None else gate + term
        w = _gelu_tanh(acc[ii * N_KEYS:(ii + 1) * N_KEYS, :], gate.astype(F32))
        o_ref[ii * N_KEYS:(ii + 1) * N_KEYS, :] = w.astype(o_ref.dtype)


def kernel(x, c, ada_w1, ada_w2, ada_b, ada_table, norm_mix, norm_ffn, norm_final, hyb_w_in, hyb_w_out, diff_lam, diff_subln, sgu_ln_g, sgu_ln_b, sgu_w_s, sgu_b_s, ssd_w_in, ssd_conv_w, ssd_conv_b, ssd_dt_bias, ssd_a_log, ssd_d, ssd_norm, ssd_w_out, peer_w_q, peer_keys, peer_u, peer_v):
    bsz, seq, d = x.shape
    depth = ada_table.shape[0]
    t = bsz * seq
    n_experts = peer_u.shape[1]
    d_inner = ssd_norm.shape[1]
    ssd_heads = ssd_a_log.shape[1]
    conv_dim = ssd_conv_b.shape[1]
    sgu_w = sgu_ln_g.shape[1]
    att_w = hyb_w_out.shape[1] - sgu_w
    att_heads = att_w // (2 * ATT_HD)

    mod = _ada(c, ada_w1, ada_w2, ada_b, ada_table)
    xt = x.reshape(t, d)

    hyb_in_b, hyb_out_b = hyb_w_in.astype(BF16), hyb_w_out.astype(BF16)
    ssd_in_b, ssd_out_b = ssd_w_in.astype(BF16), ssd_w_out.astype(BF16)
    peer_q_b, peer_u_b = peer_w_q.astype(BF16), peer_u.astype(BF16)
    peer_vt_b = jnp.swapaxes(peer_v.astype(BF16), 1, 2)
    peer_keys_b = peer_keys.astype(BF16)

    def gate_spec(l, col, tn):
        per = d // tn
        return lambda tm: (mod, (1, 1, tn), lambda i, j: (l * bsz + (i * tm) // seq, 0, col * per + j))

    for l in range(depth):
        h = _norm_mod(xt, norm_mix[l], mod, l * bsz, 0, 1, seq)
        if l % 2 == 0:
            e = l // 2
            lam_init = 0.8 - 0.6 * math.exp(-0.3 * l)
            qkv = _mm(h, hyb_in_b, b_layer=e, b_cols=(0, 3 * att_w), tm=1024, tn=1024, tk=d,
                      out_shape=jax.ShapeDtypeStruct((t, 3 * att_w), BF16), name="hyb_qkv")
            ug = _mm(h, hyb_in_b, b_layer=e, b_cols=(3 * att_w, 2 * sgu_w), tm=1024, tn=1024, tk=d,
                     out_shape=jax.ShapeDtypeStruct((t, 2 * sgu_w), F32), name="hyb_ug")
            a_out = _diff_attention(qkv, diff_lam[e], diff_subln[e], lam_init, bsz, seq, att_heads)
            s_out = _sgu(ug, sgu_ln_g[e], sgu_ln_b[e], sgu_w_s[e], sgu_b_s[e])
            y_in = jnp.concatenate([a_out, s_out], axis=-1)
            w_out, w_out_layer = hyb_out_b, e
        else:
            o = l // 2
            z = _mm(h, ssd_in_b, b_layer=o, b_cols=(0, d_inner), tm=1024, tn=1024, tk=d,
                    out_shape=jax.ShapeDtypeStruct((t, d_inner), F32), name="ssd_z")
            xbc = _mm(h, ssd_in_b, b_layer=o, b_cols=(d_inner, conv_dim), tm=1024, tn=1024, tk=d,
                      out_shape=jax.ShapeDtypeStruct((t, conv_dim), F32), name="ssd_xbc")
            dt_shape = jax.ShapeDtypeStruct((t, ssd_heads), F32)
            if ssd_heads % LANES == 0:
                dt_raw = _mm(h, ssd_in_b, b_layer=o, b_cols=(d_inner + conv_dim, ssd_heads),
                             tm=1024, tn=LANES, tk=d, out_shape=dt_shape, name="ssd_dt")
            else:
                dt_raw = _mm(h, ssd_in_b[o][:, d_inner + conv_dim:], tm=1024, tn=ssd_heads,
                             tk=d, out_shape=dt_shape, name="ssd_dt")
            xc = _conv_silu(xbc, ssd_conv_w[o], ssd_conv_b[o], bsz, seq)
            y_in = _ssd_scan(xc, z, dt_raw, ssd_dt_bias[o], ssd_a_log[o], ssd_d[o], ssd_norm[o],
                             bsz, seq, d_inner)
            w_out, w_out_layer = ssd_out_b, o
        tm, tn = _tile(1024, seq), _tile(1024, d)
        xt = _mm(y_in, w_out, b_layer=w_out_layer, tm=tm, tn=tn, tk=2048,
                 out_shape=jax.ShapeDtypeStruct((t, d), F32),
                 extras=[(xt, (tm, tn), lambda i, j: (i, j)), gate_spec(l, 2, tn)(tm)],
                 epilogue=_resid_epilogue, name="mix_out")

        h = _norm_mod(xt, norm_ffn[l], mod, l * bsz, 3, 4, seq)
        q = _mm(h, peer_q_b, b_layer=l, tm=1024, tn=1024, tk=d,
                out_shape=jax.ShapeDtypeStruct((t, peer_w_q.shape[2]), BF16), name="peer_q")
        lk, r2, e1, e2 = _peer_route(q, peer_keys_b[l])
        te, tt = _tile(1024, n_experts), _tile(512, t)
        ni1 = te // N_KEYS
        head_blk = (PEER_HEADS, ni1, tt)
        full_blk = (PEER_HEADS, N_KEYS, tt)
        wt = _mm(peer_u_b, h, a_layer=l, nt=True, tm=te, tn=tt, tk=d, n_outer=True,
                 out_shape=jax.ShapeDtypeStruct((n_experts, t), BF16),
                 extras=[(lk, head_blk, lambda i, j: (0, i, j)),
                         (e1, head_blk, lambda i, j: (0, i, j)),
                         (r2, full_blk, lambda i, j: (0, 0, j)),
                         (e2, full_blk, lambda i, j: (0, 0, j))],
                 epilogue=_peer_gate_epilogue, name="peer_gate")
        td, tt = _tile(1024, d), _tile(1024, seq)
        xt = _mm(peer_vt_b, wt, a_layer=l, tm=td, tn=tt, tk=2048,
                 out_shape=jax.ShapeDtypeStruct((t, d), F32),
                 out_block=(tt, td), out_index=lambda i, j: (j, i),
                 extras=[(xt, (tt, td), lambda i, j: (j, i)),
                         (mod, (1, 1, td), lambda i, j, l=l, tt=tt, td=td:
                          (l * bsz + (j * tt) // seq, 0, 5 * (d // td) + i))],
                 epilogue=_resid_t_epilogue, name="peer_out")

    return _final_norm(xt, norm_final).reshape(bsz, seq, d)
```

```python
import functools
import math

import jax
import jax.numpy as jnp
import numpy as np
from jax import lax
from jax.experimental import pallas as pl
from jax.experimental.pallas import tpu as pltpu

F32 = jnp.float32
BF16 = jnp.bfloat16
NORM_EPS = 1e-6
LANES = 128
VMEM_LIMIT = 56 * 1024 * 1024
NEG = -1e30

ATT_HD = 64
SSD_HD = 64
SSD_GROUPS = 8
SSD_STATE = 128
CONV_K = 4
CHUNK = 128
PEER_HEADS = 8
N_KEYS = 128
PEER_TOPK = 16
ADA_N_MOD = 6


def _cparams(sem):
    return pltpu.CompilerParams(dimension_semantics=sem, vmem_limit_bytes=VMEM_LIMIT)


def _tile(pref, dim):
    t = min(pref, dim)
    while dim % t:
        t -= LANES
        assert t > 0, (pref, dim)
    return t


def _split2(a):
    hi = a.astype(BF16)
    lo = (a - hi.astype(F32)).astype(BF16)
    return hi, lo


def _split3(a):
    hi = a.astype(BF16)
    r = a - hi.astype(F32)
    mid = r.astype(BF16)
    lo = (r - mid.astype(F32)).astype(BF16)
    return hi, mid, lo


def _dot(a, b):
    return jnp.dot(a, b, preferred_element_type=F32)


def _dot_nt(a, b):
    return lax.dot_general(a, b, (((1,), (1,)), ((), ())), preferred_element_type=F32)


def _mm_body(*refs, nk, nt, n_extra, epilogue):
    a_ref, b_ref = refs[0], refs[1]
    extra = refs[2:2 + n_extra]
    o_ref = refs[2 + n_extra]
    dot = _dot_nt if nt else _dot
    if nk == 1:
        epilogue(dot(a_ref[...], b_ref[...]), o_ref, *extra)
        return
    acc_ref = refs[3 + n_extra]
    k = pl.program_id(2)

    @pl.when(k == 0)
    def _():
        acc_ref[...] = jnp.zeros(acc_ref.shape, F32)

    acc_ref[...] += dot(a_ref[...], b_ref[...])

    @pl.when(k == nk - 1)
    def _():
        epilogue(acc_ref[...], o_ref, *extra)


def _store_epilogue(acc, o_ref):
    o_ref[...] = acc.astype(o_ref.dtype)


def _mm(a, b, *, nt=False, tm, tn, tk, out_shape, out_block=None, out_index=None,
        extras=(), epilogue=_store_epilogue, n_outer=False, a_layer=None, b_layer=None,
        b_cols=None, name="mm"):
    m, kdim = a.shape[-2:]
    col0, n = b_cols if b_cols is not None else (0, b.shape[-2] if nt else b.shape[-1])
    tm, tn, tk = _tile(tm, m), _tile(tn, n), _tile(tk, kdim)
    while col0 % tn or n % tn:
        tn -= LANES
        assert tn > 0, (col0, n)
    coff = col0 // tn
    nm, nn, nk = m // tm, n // tn, kdim // tk
    if n_outer:
        grid = (nn, nm, nk)
        ij = lambda g0, g1: (g1, g0)
    else:
        grid = (nm, nn, nk)
        ij = lambda g0, g1: (g0, g1)

    def wrap(fn):
        return lambda g0, g1, k: fn(*ij(g0, g1), k)

    def spec(block, index, layer):
        if layer is None:
            return pl.BlockSpec(block, wrap(index))
        return pl.BlockSpec((None,) + block, wrap(lambda i, j, k: (layer,) + index(i, j, k)))

    a_spec = spec((tm, tk), lambda i, j, k: (i, k), a_layer)
    if nt:
        b_spec = spec((tn, tk), lambda i, j, k: (j, k), b_layer)
    else:
        b_spec = spec((tk, tn), lambda i, j, k: (k, coff + j), b_layer)
    extra_specs = [pl.BlockSpec(blk, wrap(lambda i, j, k, f=f: f(i, j))) for _, blk, f in extras]
    if out_block is None:
        out_block, out_index = (tm, tn), (lambda i, j: (i, j))
    o_spec = pl.BlockSpec(out_block, wrap(lambda i, j, k: out_index(i, j)))
    scratch = [pltpu.VMEM((tm, tn), F32)] if nk > 1 else []
    body = functools.partial(_mm_body, nk=nk, nt=nt, n_extra=len(extras), epilogue=epilogue)
    return pl.pallas_call(
        body, out_shape=out_shape, grid=grid,
        in_specs=[a_spec, b_spec] + extra_specs, out_specs=o_spec,
        scratch_shapes=scratch, name=name,
        compiler_params=_cparams(("parallel", "parallel", "arbitrary")),
    )(a, b, *[e[0] for e in extras])


def _resid_epilogue(acc, o_ref, x_ref, g_ref):
    o_ref[...] = x_ref[...] + g_ref[0] * acc


def _resid_t_epilogue(acc, o_ref, x_ref, g_ref):
    o_ref[...] = x_ref[...] + g_ref[0] * acc.T


def _dot_split(a, b):
    a_hi, a_lo = _split2(a)
    b_hi, b_lo = _split2(b)
    return _dot(a_hi, b_hi) + _dot(a_hi, b_lo) + _dot(a_lo, b_hi)


def _ada_body(c_ref, w1_ref, w2_ref, b_ref, tab_ref, o_ref):
    c = c_ref[...]
    t = c * jax.nn.sigmoid(c)
    t1 = _dot_split(t, w1_ref[...])
    t0 = _dot_split(t1, w2_ref[...]) + b_ref[...]
    for l in range(o_ref.shape[0]):
        o_ref[l] = t0 + tab_ref[l:l + 1, :]


def _ada(c, w1, w2, b, table):
    bsz, d = c.shape
    depth, n = table.shape
    rank = w1.shape[1]
    bp = 8 * ((bsz + 7) // 8)
    cp = jnp.zeros((bp, d), F32).at[:bsz].set(c)
    tn = _tile(2048, d)
    out = pl.pallas_call(
        _ada_body, out_shape=jax.ShapeDtypeStruct((depth, bp, n), F32), grid=(n // tn,),
        in_specs=[pl.BlockSpec((bp, d), lambda j: (0, 0)),
                  pl.BlockSpec((d, rank), lambda j: (0, 0)),
                  pl.BlockSpec((rank, tn), lambda j: (0, j)),
                  pl.BlockSpec((1, tn), lambda j: (0, j)),
                  pl.BlockSpec((depth, tn), lambda j: (0, j))],
        out_specs=pl.BlockSpec((depth, bp, tn), lambda j: (0, 0, j)),
        name="ada", compiler_params=_cparams(("parallel",)),
    )(cp, w1, w2, b.reshape(1, n), table)
    return out[:, :bsz].reshape(depth * bsz, 1, n)


def _rms(x, g):
    return x * lax.rsqrt(jnp.mean(x * x, axis=-1, keepdims=True) + NORM_EPS) * g


def _norm_mod_body(x_ref, g_ref, sc_ref, sh_ref, o_ref):
    y = _rms(x_ref[...], g_ref[...])
    o_ref[...] = (y * (1.0 + sc_ref[0]) + sh_ref[0]).astype(o_ref.dtype)


def _norm_body(x_ref, g_ref, o_ref):
    o_ref[...] = _rms(x_ref[...], g_ref[...]).astype(o_ref.dtype)


def _norm_mod(x, g, mod, row0, sh_col, sc_col, seq):
    t, d = x.shape
    tm = _tile(512, seq)
    return pl.pallas_call(
        _norm_mod_body, out_shape=jax.ShapeDtypeStruct((t, d), BF16), grid=(t // tm,),
        in_specs=[pl.BlockSpec((tm, d), lambda i: (i, 0)),
                  pl.BlockSpec((1, d), lambda i: (0, 0)),
                  pl.BlockSpec((1, 1, d), lambda i: (row0 + (i * tm) // seq, 0, sc_col)),
                  pl.BlockSpec((1, 1, d), lambda i: (row0 + (i * tm) // seq, 0, sh_col))],
        out_specs=pl.BlockSpec((tm, d), lambda i: (i, 0)),
        name="norm_mod", compiler_params=_cparams(("parallel",)),
    )(x, g.reshape(1, d), mod, mod)


def _final_norm(x, g):
    t, d = x.shape
    tm = _tile(512, t)
    return pl.pallas_call(
        _norm_body, out_shape=jax.ShapeDtypeStruct((t, d), F32), grid=(t // tm,),
        in_specs=[pl.BlockSpec((tm, d), lambda i: (i, 0)),
                  pl.BlockSpec((1, d), lambda i: (0, 0))],
        out_specs=pl.BlockSpec((tm, d), lambda i: (i, 0)),
        name="final_norm", compiler_params=_cparams(("parallel",)),
    )(x, g.reshape(1, d))


def _attn_body(it_ref, jt_ref, q_ref, k_ref, v_ref, lam_ref, g_ref, o_ref,
               q_s, m_s, l_s, a_s, *, tq, tk, hb, lam_init):
    i = it_ref[pl.program_id(2)]
    j = jt_ref[pl.program_id(2)]
    vd = 2 * ATT_HD

    @pl.when(j == 0)
    def _():
        q = q_ref[...] * (ATT_HD ** -0.5)
        lane = lax.broadcasted_iota(jnp.int32, (tq, vd), 1)
        for hh in range(hb):
            qh = q[:, hh * vd:(hh + 1) * vd]
            q_s[hh, :tq, :] = jnp.where(lane < ATT_HD, qh, 0).astype(BF16)
            q_s[hh, tq:, :] = jnp.where(lane >= ATT_HD, qh, 0).astype(BF16)
        m_s[...] = jnp.full(m_s.shape, NEG, F32)
        l_s[...] = jnp.zeros(l_s.shape, F32)
        a_s[...] = jnp.zeros(a_s.shape, F32)

    def step(masked):
        if masked:
            krow = lax.broadcasted_iota(jnp.int32, (tk, 2 * tq), 0)
            qcol = lax.broadcasted_iota(jnp.int32, (tk, 2 * tq), 1)
            causal = krow <= jnp.where(qcol >= tq, qcol - tq, qcol)
        for hh in range(hb):
            k = k_ref[:, hh * vd:(hh + 1) * vd]
            v = v_ref[:, hh * vd:(hh + 1) * vd]
            s = _dot_nt(k, q_s[hh])
            if masked:
                s = jnp.where(causal, s, NEG)
            m_old = m_s[hh]
            m_new = jnp.maximum(m_old, jnp.max(s, axis=0, keepdims=True))
            alpha = jnp.exp(m_old - m_new)
            p = jnp.exp(s - m_new)
            l_s[hh] = alpha * l_s[hh] + jnp.sum(p, axis=0, keepdims=True)
            pv = lax.dot_general(v, p.astype(BF16), (((0,), (0,)), ((), ())),
                                 preferred_element_type=F32)
            a_s[hh] = alpha * a_s[hh] + pv
            m_s[hh] = m_new

    @pl.when(j < i)
    def _():
        step(False)

    @pl.when(j == i)
    def _():
        step(True)
        lp = lam_ref[...]
        lam = (jnp.exp(jnp.sum(lp[0:1] * lp[1:2], axis=-1, keepdims=True))
               - jnp.exp(jnp.sum(lp[2:3] * lp[3:4], axis=-1, keepdims=True)) + lam_init)
        for hh in range(hb):
            w = a_s[hh] / l_s[hh]
            o = w[:, :tq] - lam * w[:, tq:]
            o = o * lax.rsqrt(jnp.mean(o * o, axis=0, keepdims=True) + NORM_EPS) * g_ref[...]
            o_ref[:, hh * vd:(hh + 1) * vd] = (o * (1.0 - lam_init)).T.astype(o_ref.dtype)


def _diff_attention(qkv, lam_p, subln_g, lam_init, bsz, seq, heads):
    t = qkv.shape[0]
    vd = 2 * ATT_HD
    hb = 4 if heads % 4 == 0 else (2 if heads % 2 == 0 else 1)
    hblk = heads // hb
    tq = tk = _tile(512, seq)
    nq = seq // tq
    pairs = [(i, j) for i in range(nq) for j in range(i + 1)]
    i_tab = jnp.asarray([p[0] for p in pairs], jnp.int32)
    j_tab = jnp.asarray([p[1] for p in pairs], jnp.int32)
    body = functools.partial(_attn_body, tq=tq, tk=tk, hb=hb, lam_init=lam_init)
    grid_spec = pltpu.PrefetchScalarGridSpec(
        num_scalar_prefetch=2, grid=(bsz, hblk, len(pairs)),
        in_specs=[pl.BlockSpec((tq, hb * vd), lambda b, h, p, it, jt: (b * nq + it[p], h)),
                  pl.BlockSpec((tk, hb * vd), lambda b, h, p, it, jt: (b * nq + jt[p], hblk + h)),
                  pl.BlockSpec((tk, hb * vd), lambda b, h, p, it, jt: (b * nq + jt[p], 2 * hblk + h)),
                  pl.BlockSpec((4, ATT_HD), lambda b, h, p, it, jt: (0, 0)),
                  pl.BlockSpec((vd, 1), lambda b, h, p, it, jt: (0, 0))],
        out_specs=pl.BlockSpec((tq, hb * vd), lambda b, h, p, it, jt: (b * nq + it[p], h)),
        scratch_shapes=[pltpu.VMEM((hb, 2 * tq, vd), BF16), pltpu.VMEM((hb, 1, 2 * tq), F32),
                        pltpu.VMEM((hb, 1, 2 * tq), F32), pltpu.VMEM((hb, vd, 2 * tq), F32)])
    return pl.pallas_call(
        body, out_shape=jax.ShapeDtypeStruct((t, heads * vd), BF16), grid_spec=grid_spec,
        name="diff_attn",
        compiler_params=_cparams(("parallel", "parallel", "arbitrary")),
    )(i_tab, j_tab, qkv, qkv, qkv, lam_p, subln_g.reshape(vd, 1))


def _sgu_body(u_ref, v_ref, lng_ref, lnb_ref, w_ref, bias_ref, o_ref, *, groups, nchunk):
    u = jax.nn.gelu(u_ref[...])
    v = jax.nn.gelu(v_ref[...])
    mu = jnp.mean(v, axis=-1, keepdims=True)
    vc = v - mu
    var = jnp.mean(vc * vc, axis=-1, keepdims=True)
    vn = (vc * lax.rsqrt(var + NORM_EPS) * lng_ref[...] + lnb_ref[...]).astype(BF16)
    ri = lax.broadcasted_iota(jnp.int32, (CHUNK, CHUNK), 0)
    ci = lax.broadcasted_iota(jnp.int32, (CHUNK, CHUNK), 1)
    causal = ci <= ri
    for g in range(groups):
        cs = slice(g * LANES, (g + 1) * LANES)
        wg = jnp.where(causal, w_ref[g], 0.0).astype(BF16)
        bias = bias_ref[:, cs]
        for c in range(nchunk):
            rs = slice(c * CHUNK, (c + 1) * CHUNK)
            mixed = _dot(wg, vn[rs, cs]) + bias
            o_ref[rs, cs] = (u[rs, cs] * mixed).astype(o_ref.dtype)


def _sgu(ug, ln_g, ln_b, w_s, b_s):
    t, w2 = ug.shape
    w = w2 // 2
    groups = w // LANES
    ts = _tile(256, t)
    bias = jnp.repeat(b_s.T, LANES, axis=1)
    body = functools.partial(_sgu_body, groups=groups, nchunk=ts // CHUNK)
    return pl.pallas_call(
        body, out_shape=jax.ShapeDtypeStruct((t, w), BF16), grid=(t // ts,),
        in_specs=[pl.BlockSpec((ts, w), lambda i: (i, 0)),
                  pl.BlockSpec((ts, w), lambda i: (i, 1)),
                  pl.BlockSpec((1, w), lambda i: (0, 0)),
                  pl.BlockSpec((1, w), lambda i: (0, 0)),
                  pl.BlockSpec((groups, CHUNK, CHUNK), lambda i: (0, 0, 0)),
                  pl.BlockSpec((CHUNK, w), lambda i: (0, 0))],
        out_specs=pl.BlockSpec((ts, w), lambda i: (i, 0)),
        name="sgu", compiler_params=_cparams(("parallel",)),
    )(ug, ug, ln_g.reshape(1, w), ln_b.reshape(1, w), w_s, bias)


def _conv_body(x_ref, w_ref, b_ref, o_ref, prev_s):
    @pl.when(pl.program_id(2) == 0)
    def _():
        prev_s[...] = jnp.zeros(prev_s.shape, F32)

    x = x_ref[...]
    prev = prev_s[...]
    row = lax.broadcasted_iota(jnp.int32, x.shape, 0)
    acc = x * w_ref[CONV_K - 1:CONV_K, :] + b_ref[...]
    for k in range(1, CONV_K):
        xs = jnp.where(row < k, pltpu.roll(prev, k, 0), pltpu.roll(x, k, 0))
        acc = acc + xs * w_ref[CONV_K - 1 - k:CONV_K - k, :]
    o_ref[...] = acc * jax.nn.sigmoid(acc)
    prev_s[...] = x


def _conv_silu(xbc, w, b, bsz, seq):
    t, c = xbc.shape
    ts = _tile(512, seq)
    tc = _tile(2048, c)
    ns = seq // ts
    return pl.pallas_call(
        _conv_body, out_shape=jax.ShapeDtypeStruct((t, c), F32), grid=(c // tc, bsz, ns),
        in_specs=[pl.BlockSpec((ts, tc), lambda ci, bi, si: (bi * ns + si, ci)),
                  pl.BlockSpec((CONV_K, tc), lambda ci, bi, si: (0, ci)),
                  pl.BlockSpec((1, tc), lambda ci, bi, si: (0, ci))],
        out_specs=pl.BlockSpec((ts, tc), lambda ci, bi, si: (bi * ns + si, ci)),
        scratch_shapes=[pltpu.VMEM((ts, tc), F32)],
        name="conv_silu", compiler_params=_cparams(("parallel", "parallel", "arbitrary")),
    )(xbc, w, b.reshape(1, c))


def _softplus(x):
    return jnp.maximum(x, 0.0) + jnp.log1p(jnp.exp(-jnp.abs(x)))


def _ssd_body(x_ref, b_ref, c_ref, z_ref, dta_ref, dtb_ref, ba_ref, bb_ref, ala_ref, alb_ref,
              d_ref, ng_ref, o_ref, state_s, y_s, *, hpg):
    n = pl.program_id(2)
    L = CHUNK
    gw = hpg * SSD_HD

    @pl.when(n == 0)
    def _():
        state_s[...] = jnp.zeros(state_s.shape, F32)

    dt = _softplus(dta_ref[0] + ba_ref[0])
    dt_t = _softplus(dtb_ref[0] + bb_ref[0])
    da = dt * (-jnp.exp(ala_ref[0]))
    da_t = dt_t * (-jnp.exp(alb_ref[0]))
    ri = lax.broadcasted_iota(jnp.int32, (L, L), 0)
    ci = lax.broadcasted_iota(jnp.int32, (L, L), 1)
    causal = ci <= ri
    tri = causal.astype(BF16)
    tri_t = (ri <= ci).astype(BF16)
    acs = sum(_dot(tri, p) for p in _split3(da))
    acs_t = sum(_dot(p, tri_t) for p in _split3(da_t))
    acs_last = acs[L - 1:L, :]

    head_of_col = lax.broadcasted_iota(jnp.int32, (hpg, gw), 1) // SSD_HD
    expand = (head_of_col == lax.broadcasted_iota(jnp.int32, (hpg, gw), 0)).astype(BF16)

    def widen(v):
        return sum(_dot(p, expand) for p in _split2(v))

    dt_e = widen(dt)
    dec_e = widen(jnp.exp(acs))
    tail_e = widen(jnp.exp(acs_last - acs))

    x = x_ref[...]
    xdt = x * dt_e
    xdt_b = xdt.astype(BF16)
    bm = b_ref[...]
    cm_b = c_ref[...].astype(BF16)
    bm_b = bm.astype(BF16)
    cb = _dot_nt(cm_b, bm_b)
    state = state_s[...]
    y_s[...] = _dot(cm_b, state.astype(BF16)) * dec_e

    lane = lax.broadcasted_iota(jnp.int32, (L, LANES), 1)
    for pair in range(hpg // 2):
        cs = slice(pair * LANES, (pair + 1) * LANES)
        xp = xdt_b[:, cs]
        acc = None
        for half in range(2):
            r = 2 * pair + half
            seg = acs[:, r:r + 1] - acs_t[r:r + 1, :]
            decay = jnp.exp(jnp.where(causal, seg, NEG))
            mm = (cb * decay).astype(BF16)
            keep = (lane < SSD_HD) if half == 0 else (lane >= SSD_HD)
            part = _dot(mm, jnp.where(keep, xp, 0))
            acc = part if acc is None else acc + part
        y_s[:, cs] += acc

    state_s[...] = state * dec_e[L - 1:L, :] + _dot(bm.T.astype(BF16), (tail_e * xdt).astype(BF16))

    y = y_s[...] + x * d_ref[...]
    z = z_ref[...].astype(F32)
    y = y * (z * jax.nn.sigmoid(z))
    o_ref[...] = _rms(y, ng_ref[...]).astype(o_ref.dtype)


def _ssd_scan(xc, z, dt_raw, dt_bias, a_log, d_skip, norm_g, bsz, seq, d_inner):
    t = xc.shape[0]
    g, n_state = SSD_GROUPS, SSD_STATE
    heads = dt_raw.shape[1]
    hpg = heads // g
    gw = hpg * SSD_HD
    assert gw % LANES == 0 and hpg % 2 == 0
    nc = seq // CHUNK
    dta = dt_raw.reshape(t, g, hpg).transpose(1, 0, 2)
    dtb = dta.transpose(0, 2, 1)
    ba = dt_bias.reshape(g, 1, hpg)
    bb = dt_bias.reshape(g, hpg, 1)
    ala = a_log.reshape(g, 1, hpg)
    alb = a_log.reshape(g, hpg, 1)
    d_e = jnp.repeat(d_skip, SSD_HD).reshape(1, d_inner)
    xcol = gw // LANES
    b0 = d_inner // n_state
    c0 = (d_inner + g * n_state) // n_state
    row = lambda bi, ni: bi * nc + ni
    body = functools.partial(_ssd_body, hpg=hpg)
    return pl.pallas_call(
        body, out_shape=jax.ShapeDtypeStruct((t, d_inner), BF16), grid=(bsz, g, nc),
        in_specs=[pl.BlockSpec((CHUNK, gw), lambda bi, gi, ni: (row(bi, ni), gi)),
                  pl.BlockSpec((CHUNK, n_state), lambda bi, gi, ni: (row(bi, ni), b0 + gi)),
                  pl.BlockSpec((CHUNK, n_state), lambda bi, gi, ni: (row(bi, ni), c0 + gi)),
                  pl.BlockSpec((CHUNK, gw), lambda bi, gi, ni: (row(bi, ni), gi)),
                  pl.BlockSpec((1, CHUNK, hpg), lambda bi, gi, ni: (gi, row(bi, ni), 0)),
                  pl.BlockSpec((1, hpg, CHUNK), lambda bi, gi, ni: (gi, 0, row(bi, ni))),
                  pl.BlockSpec((1, 1, hpg), lambda bi, gi, ni: (gi, 0, 0)),
                  pl.BlockSpec((1, hpg, 1), lambda bi, gi, ni: (gi, 0, 0)),
                  pl.BlockSpec((1, 1, hpg), lambda bi, gi, ni: (gi, 0, 0)),
                  pl.BlockSpec((1, hpg, 1), lambda bi, gi, ni: (gi, 0, 0)),
                  pl.BlockSpec((1, gw), lambda bi, gi, ni: (0, gi)),
                  pl.BlockSpec((1, gw), lambda bi, gi, ni: (0, gi))],
        out_specs=pl.BlockSpec((CHUNK, gw), lambda bi, gi, ni: (row(bi, ni), gi)),
        scratch_shapes=[pltpu.VMEM((n_state, gw), F32), pltpu.VMEM((CHUNK, gw), F32)],
        name="ssd_scan", compiler_params=_cparams(("parallel", "parallel", "arbitrary")),
    )(xc, xc, xc, z, dta, dtb, ba, bb, ala, alb, d_e, norm_g.reshape(1, d_inner))


def _topk_rank(s, k_top, exact):
    n = s.shape[0]
    iota = lax.broadcasted_iota(jnp.int32, s.shape, 0).astype(F32)
    rank = jnp.full(s.shape, float(k_top), F32)
    vals = []
    for k in range(k_top):
        m = jnp.max(s, axis=0, keepdims=True)
        hit = s == m
        if exact:
            idx = jnp.min(jnp.where(hit, iota, float(n)), axis=0, keepdims=True)
            hit = iota == idx
        rank = jnp.where(hit, float(k), rank)
        s = jnp.where(hit, -jnp.inf, s)
        vals.append(m)
    return rank, jnp.concatenate(vals, axis=0)


_CANDS = [(a, b) for a in range(PEER_TOPK) for b in range(PEER_TOPK // (a + 1))]
N_CAND = 64


def _cand_tables():
    pa = np.zeros((N_CAND, PEER_TOPK), np.float32)
    pb = np.zeros((N_CAND, PEER_TOPK), np.float32)
    for r, (a, b) in enumerate(_CANDS):
        pa[r, a] = 1.0
        pb[r, b] = 1.0
    return jnp.asarray(pa, BF16), jnp.asarray(pb, BF16), jnp.asarray(pa.T, BF16)


def _pick_rows(onehot, v):
    hi, mid, lo = _split3(v)
    return (_dot(onehot, hi) + _dot(onehot, mid)) + _dot(onehot, lo)


def _route_compute(s1, s2, pa, pb, ga, exact):
    kt = PEER_TOPK
    r1, v1 = _topk_rank(s1, kt, exact)
    r2, v2 = _topk_rank(s2, kt, exact)
    c1 = _pick_rows(pa, v1)
    c2 = _pick_rows(pb, v2)
    row = lax.broadcasted_iota(jnp.int32, c1.shape, 0)
    cand = jnp.where(row < len(_CANDS), c1 + c2, -jnp.inf)
    crank, _ = _topk_rank(cand, kt, exact)
    sel = crank < float(kt)
    gates = jnp.where(sel, jnp.exp(c1 - v1[0:1]) * jnp.exp(c2 - v2[0:1]), 0.0)
    zsum = jnp.sum(gates, axis=0, keepdims=True)
    count = _dot(ga, sel.astype(F32).astype(BF16))
    lk = jnp.zeros(s1.shape, F32)
    for a in range(kt):
        lk = jnp.where(r1 == float(a), count[a:a + 1], lk)
    e1 = jnp.exp(s1 - v1[0:1])
    e2 = jnp.exp(s2 - v2[0:1]) / zsum * 0.5
    nsel = sum(jnp.sum((r < float(kt)).astype(F32), axis=0, keepdims=True) for r in (r1, r2, crank))
    return (lk, r2, e1, e2), nsel


def _route_body(q_ref, k_ref, pa_ref, pb_ref, ga_ref, lk_ref, r2_ref, e1_ref, e2_ref):
    q = q_ref[...]
    s1 = _dot_nt(k_ref[0, 0], q[:, :N_KEYS])
    s2 = _dot_nt(k_ref[0, 1], q[:, N_KEYS:])
    tabs = (pa_ref[...], pb_ref[...], ga_ref[...])

    def store(outs):
        for ref, val in zip((lk_ref, r2_ref, e1_ref, e2_ref), outs):
            ref[0] = val.astype(ref.dtype)

    outs, nsel = _route_compute(s1, s2, *tabs, exact=False)
    tied = jnp.max(jnp.abs(nsel - 3.0 * PEER_TOPK)) > 0.0

    @pl.when(jnp.logical_not(tied))
    def _():
        store(outs)

    @pl.when(tied)
    def _():
        store(_route_compute(s1, s2, *tabs, exact=True)[0])


def _peer_route(q, keys):
    t = q.shape[0]
    tt = _tile(512, t)
    f32_shp = jax.ShapeDtypeStruct((PEER_HEADS, N_KEYS, t), F32)
    bf16_shp = jax.ShapeDtypeStruct((PEER_HEADS, N_KEYS, t), BF16)
    ospec = pl.BlockSpec((1, N_KEYS, tt), lambda i, h: (h, 0, i))
    tab_spec = lambda shape: pl.BlockSpec(shape, lambda i, h: (0, 0))
    pa, pb, ga = _cand_tables()
    return pl.pallas_call(
        _route_body, out_shape=(f32_shp, bf16_shp, f32_shp, bf16_shp), grid=(t // tt, PEER_HEADS),
        in_specs=[pl.BlockSpec((tt, 2 * N_KEYS), lambda i, h: (i, h)),
                  pl.BlockSpec((1, 2, N_KEYS, N_KEYS), lambda i, h: (h, 0, 0, 0)),
                  tab_spec(pa.shape), tab_spec(pb.shape), tab_spec(ga.shape)],
        out_specs=(ospec, ospec, ospec, ospec),
        name="peer_route", compiler_params=_cparams(("parallel", "parallel")),
    )(q, keys, pa, pb, ga)


def _gelu_tanh(x, half_scale):
    c0 = math.sqrt(2.0 / math.pi)
    inner = x * (c0 + (c0 * 0.044715) * (x * x))
    return (x * half_scale) * (1.0 + jnp.tanh(inner))


def _peer_gate_epilogue(acc, o_ref, lk_ref, e1_ref, r2_ref, e2_ref):
    for ii in range(acc.shape[0] // N_KEYS):
        gate = None
        for h in range(PEER_HEADS):
            lk = lk_ref[h, ii:ii + 1, :].astype(BF16)
            e1 = e1_ref[h, ii:ii + 1, :].astype(BF16)
            term = e1 * jnp.where(r2_ref[h] < lk, e2_ref[h], jnp.zeros((), BF16))
            gate = term if gate is None else gate + term
        w = _gelu_tanh(acc[ii * N_KEYS:(ii + 1) * N_KEYS, :], gate.astype(F32))
        o_ref[ii * N_KEYS:(ii + 1) * N_KEYS, :] = w.astype(o_ref.dtype)


def kernel(x, c, ada_w1, ada_w2, ada_b, ada_table, norm_mix, norm_ffn, norm_final, hyb_w_in, hyb_w_out, diff_lam, diff_subln, sgu_ln_g, sgu_ln_b, sgu_w_s, sgu_b_s, ssd_w_in, ssd_conv_w, ssd_conv_b, ssd_dt_bias, ssd_a_log, ssd_d, ssd_norm, ssd_w_out, peer_w_q, peer_keys, peer_u, peer_v):
    bsz, seq, d = x.shape
    depth = ada_table.shape[0]
    t = bsz * seq
    n_experts = peer_u.shape[1]
    d_inner = ssd_norm.shape[1]
    ssd_heads = ssd_a_log.shape[1]
    conv_dim = ssd_conv_b.shape[1]
    sgu_w = sgu_ln_g.shape[1]
    att_w = hyb_w_out.shape[1] - sgu_w
    att_heads = att_w // (2 * ATT_HD)

    mod = _ada(c, ada_w1, ada_w2, ada_b, ada_table)
    xt = x.reshape(t, d)

    hyb_in_b, hyb_out_b = hyb_w_in.astype(BF16), hyb_w_out.astype(BF16)
    ssd_in_b, ssd_out_b = ssd_w_in.astype(BF16), ssd_w_out.astype(BF16)
    peer_q_b, peer_u_b = peer_w_q.astype(BF16), peer_u.astype(BF16)
    peer_vt_b = jnp.swapaxes(peer_v.astype(BF16), 1, 2)
    peer_keys_b = peer_keys.astype(BF16)

    def gate_spec(l, col, tn):
        per = d // tn
        return lambda tm: (mod, (1, 1, tn), lambda i, j: (l * bsz + (i * tm) // seq, 0, col * per + j))

    for l in range(depth):
        h = _norm_mod(xt, norm_mix[l], mod, l * bsz, 0, 1, seq)
        if l % 2 == 0:
            e = l // 2
            lam_init = 0.8 - 0.6 * math.exp(-0.3 * l)
            qkv = _mm(h, hyb_in_b, b_layer=e, b_cols=(0, 3 * att_w), tm=1024, tn=1024, tk=d,
                      out_shape=jax.ShapeDtypeStruct((t, 3 * att_w), BF16), name="hyb_qkv")
            ug = _mm(h, hyb_in_b, b_layer=e, b_cols=(3 * att_w, 2 * sgu_w), tm=1024, tn=1024, tk=d,
                     out_shape=jax.ShapeDtypeStruct((t, 2 * sgu_w), F32), name="hyb_ug")
            a_out = _diff_attention(qkv, diff_lam[e], diff_subln[e], lam_init, bsz, seq, att_heads)
            s_out = _sgu(ug, sgu_ln_g[e], sgu_ln_b[e], sgu_w_s[e], sgu_b_s[e])
            y_in = jnp.concatenate([a_out, s_out], axis=-1)
            w_out, w_out_layer = hyb_out_b, e
        else:
            o = l // 2
            z = _mm(h, ssd_in_b, b_layer=o, b_cols=(0, d_inner), tm=1024, tn=1024, tk=d,
                    out_shape=jax.ShapeDtypeStruct((t, d_inner), F32), name="ssd_z")
            xbc = _mm(h, ssd_in_b, b_layer=o, b_cols=(d_inner, conv_dim), tm=1024, tn=1024, tk=d,
                      out_shape=jax.ShapeDtypeStruct((t, conv_dim), F32), name="ssd_xbc")
            dt_shape = jax.ShapeDtypeStruct((t, ssd_heads), F32)
            if ssd_heads % LANES == 0:
                dt_raw = _mm(h, ssd_in_b, b_layer=o, b_cols=(d_inner + conv_dim, ssd_heads),
                             tm=1024, tn=LANES, tk=d, out_shape=dt_shape, name="ssd_dt")
            else:
                dt_raw = _mm(h, ssd_in_b[o][:, d_inner + conv_dim:], tm=1024, tn=ssd_heads,
                             tk=d, out_shape=dt_shape, name="ssd_dt")
            xc = _conv_silu(xbc, ssd_conv_w[o], ssd_conv_b[o], bsz, seq)
            y_in = _ssd_scan(xc, z, dt_raw, ssd_dt_bias[o], ssd_a_log[o], ssd_d[o], ssd_norm[o],
                             bsz, seq, d_inner)
            w_out, w_out_layer = ssd_out_b, o
        tm, tn = _tile(1024, seq), _tile(1024, d)
        xt = _mm(y_in, w_out, b_layer=w_out_layer, tm=tm, tn=tn, tk=2048,
                 out_shape=jax.ShapeDtypeStruct((t, d), F32),
                 extras=[(xt, (tm, tn), lambda i, j: (i, j)), gate_spec(l, 2, tn)(tm)],
                 epilogue=_resid_epilogue, name="mix_out")

        h = _norm_mod(xt, norm_ffn[l], mod, l * bsz, 3, 4, seq)
        q = _mm(h, peer_q_b, b_layer=l, tm=1024, tn=1024, tk=d,
                out_shape=jax.ShapeDtypeStruct((t, peer_w_q.shape[2]), BF16), name="peer_q")
        lk, r2, e1, e2 = _peer_route(q, peer_keys_b[l])
        te, tt = _tile(1024, n_experts), _tile(512, t)
        ni1 = te // N_KEYS
        head_blk = (PEER_HEADS, ni1, tt)
        full_blk = (PEER_HEADS, N_KEYS, tt)
        wt = _mm(peer_u_b, h, a_layer=l, nt=True, tm=te, tn=tt, tk=d, n_outer=True,
                 out_shape=jax.ShapeDtypeStruct((n_experts, t), BF16),
                 extras=[(lk, head_blk, lambda i, j: (0, i, j)),
                         (e1, head_blk, lambda i, j: (0, i, j)),
                         (r2, full_blk, lambda i, j: (0, 0, j)),
                         (e2, full_blk, lambda i, j: (0, 0, j))],
                 epilogue=_peer_gate_epilogue, name="peer_gate")
        td, tt = _tile(1024, d), _tile(1024, seq)
        xt = _mm(peer_vt_b, wt, a_layer=l, tm=td, tn=tt, tk=2048,
                 out_shape=jax.ShapeDtypeStruct((t, d), F32),
                 out_block=(tt, td), out_index=lambda i, j: (j, i),
                 extras=[(xt, (tt, td), lambda i, j: (j, i)),
                         (mod, (1, 1, td), lambda i, j, l=l, tt=tt, td=td:
                          (l * bsz + (j * tt) // seq, 0, 5 * (d // td) + i))],
                 epilogue=_resid_t_epilogue, name="peer_out")

    return _final_norm(xt, norm_final).reshape(bsz, seq, d)
```
